```python
import jax, jax.numpy as jnp
from jax import lax
import numpy as np

D_MODEL = 1024
BATCH = 4
SEQ = 8192
DEPTH = 2

HEAD_DIM = 64
N_HEADS = D_MODEL // HEAD_DIM
D_MIX = N_HEADS * HEAD_DIM
H_A = N_HEADS // 2
H_B = N_HEADS - H_A
H_C = N_HEADS // 2
H_D = N_HEADS - H_C
D_FF = 256 * ((8 * D_MODEL + 3 * 256 - 1) // (3 * 256))
FFN_RES_WEIGHT = 0.5
Q_BLOCK = 128
MOBA_BLOCK = 256
MOBA_TOPK = 3
MOBA_Q_BLOCK = 64
DILATED_BRANCHES = ((128, 1), (512, 4), (2048, 16))
ROPE_THETA = 10000.0
RMS_EPS = 1e-6
N_EVEN = (DEPTH + 1) // 2
N_ODD = DEPTH // 2

kernel_name = 'hybrid_sb_moba_fox_dilated_macaron'


def rms_norm(x, g):
    xf = x.astype(jnp.float32)
    y = xf * lax.rsqrt(jnp.mean(xf * xf, axis=-1, keepdims=True) + RMS_EPS)
    return (y * g.astype(jnp.float32)).astype(x.dtype)


def swiglu(x, w_gate, w_up, w_down):
    return (jax.nn.silu(x @ w_gate) * (x @ w_up)) @ w_down


def rope_tables(seq):
    inv = 1.0 / (ROPE_THETA ** (jnp.arange(0, HEAD_DIM, 2, dtype=jnp.float32) / HEAD_DIM))
    ang = jnp.arange(seq, dtype=jnp.float32)[:, None] * inv[None, :]
    return jnp.cos(ang), jnp.sin(ang)


def apply_rope(x, cos, sin):
    x1, x2 = jnp.split(x, 2, axis=-1)
    c = cos.astype(x.dtype)
    s = sin.astype(x.dtype)
    return jnp.concatenate([x1 * c - x2 * s, x2 * c + x1 * s], axis=-1)


def split_heads(t, n_heads):
    b, s, _ = t.shape
    return t.reshape(b, s, n_heads, HEAD_DIM).transpose(0, 2, 1, 3)


def merge_heads(t):
    b, h, s, d = t.shape
    return t.transpose(0, 2, 1, 3).reshape(b, s, h * d)


def to_query_blocks(t, blk):
    b, h, s = t.shape[:3]
    t = t.reshape(b, h, s // blk, blk, *t.shape[3:])
    return jnp.moveaxis(t, 2, 0)


def from_query_blocks(t):
    n, b, h, blk, d = t.shape
    return jnp.moveaxis(t, 0, 2).reshape(b, h, n * blk, d)


def stick_breaking_attention(q, k, v):
    b, h, s, dh = q.shape
    scale = dh ** -0.5
    kpos = jnp.arange(s)

    def one_block(args):
        q_blk, i = args
        qpos = i * Q_BLOCK + jnp.arange(Q_BLOCK)
        strict = kpos[None, :] < qpos[:, None]
        z = jnp.einsum('bhqd,bhkd->bhqk', q_blk, k).astype(jnp.float32) * scale
        log_keep = jnp.where(strict, jax.nn.log_sigmoid(-z), 0.0)
        suffix = lax.cumsum(log_keep, axis=3, reverse=True)
        between = jnp.concatenate([suffix[..., 1:], jnp.zeros_like(suffix[..., :1])], axis=-1)
        weight = jnp.where(strict, jnp.exp(jax.nn.log_sigmoid(z) + between), 0.0)
        return jnp.einsum('bhqk,bhkd->bhqd', weight.astype(v.dtype), v)

    out = lax.map(one_block, (to_query_blocks(q, Q_BLOCK), jnp.arange(s // Q_BLOCK)))
    return from_query_blocks(out)


def moba_attention(q, k, v):
    b, h, s, dh = q.shape
    scale = dh ** -0.5
    n_kb = -(-s // MOBA_BLOCK)
    pad = n_kb * MOBA_BLOCK - s
    k_pad = jnp.pad(k, ((0, 0), (0, 0), (0, pad), (0, 0)))
    v_pad = jnp.pad(v, ((0, 0), (0, 0), (0, pad), (0, 0)))
    k_blocks = k_pad.reshape(b, h, n_kb, MOBA_BLOCK, dh)
    v_blocks = v_pad.reshape(b, h, n_kb, MOBA_BLOCK, dh)
    k_mean = jnp.mean(k_blocks.astype(jnp.float32), axis=3)
    top_k = min(MOBA_TOPK, n_kb)
    bi = jnp.arange(b)[:, None, None, None]
    hi = jnp.arange(h)[None, :, None, None]
    blk_ids = jnp.arange(n_kb)
    local = jnp.arange(MOBA_BLOCK)
    n_sel = top_k * MOBA_BLOCK

    def one_block(args):
        q_blk, i = args
        q0 = i * MOBA_Q_BLOCK
        qpos = q0 + jnp.arange(MOBA_Q_BLOCK)
        own = q0 // MOBA_BLOCK
        gate = jnp.einsum('bhqd,bhnd->bhqn', q_blk.astype(jnp.float32), k_mean)
        gate = jnp.where(blk_ids < own, gate, -jnp.inf)
        _, sel = lax.top_k(gate, top_k)
        sel_ok = jnp.arange(top_k) < own
        k_sel = k_blocks[bi, hi, sel]
        v_sel = v_blocks[bi, hi, sel]
        s_sel = jnp.einsum('bhqd,bhqnkd->bhqnk', q_blk, k_sel).astype(jnp.float32) * scale
        s_sel = jnp.where(sel_ok[:, None], s_sel, -jnp.inf).reshape(b, h, MOBA_Q_BLOCK, n_sel)
        k_own = lax.dynamic_slice_in_dim(k_pad, own * MOBA_BLOCK, MOBA_BLOCK, axis=2)
        v_own = lax.dynamic_slice_in_dim(v_pad, own * MOBA_BLOCK, MOBA_BLOCK, axis=2)
        s_own = jnp.einsum('bhqd,bhkd->bhqk', q_blk, k_own).astype(jnp.float32) * scale
        s_own = jnp.where(own * MOBA_BLOCK + local[None, :] <= qpos[:, None], s_own, -jnp.inf)
        p = jax.nn.softmax(jnp.concatenate([s_sel, s_own], axis=-1), axis=-1).astype(v.dtype)
        p_sel = p[..., :n_sel].reshape(b, h, MOBA_Q_BLOCK, top_k, MOBA_BLOCK)
        p_own = p[..., n_sel:]
        return (jnp.einsum('bhqnk,bhqnkd->bhqd', p_sel, v_sel)
                + jnp.einsum('bhqk,bhkd->bhqd', p_own, v_own))

    out = lax.map(one_block, (to_query_blocks(q, MOBA_Q_BLOCK), jnp.arange(s // MOBA_Q_BLOCK)))
    return from_query_blocks(out)


def forgetting_attention(q, k, v, log_f):
    b, h, s, dh = q.shape
    scale = dh ** -0.5
    c = jnp.cumsum(log_f, axis=-1)
    kpos = jnp.arange(s)

    def one_block(args):
        q_blk, c_blk, i = args
        qpos = i * Q_BLOCK + jnp.arange(Q_BLOCK)
        logits = (jnp.einsum('bhqd,bhkd->bhqk', q_blk, k).astype(jnp.float32) * scale
                  + c_blk[..., :, None] - c[..., None, :])
        logits = jnp.where(kpos[None, :] <= qpos[:, None], logits, -jnp.inf)
        p = jax.nn.softmax(logits, axis=-1).astype(v.dtype)
        return jnp.einsum('bhqk,bhkd->bhqd', p, v)

    out = lax.map(one_block, (to_query_blocks(q, Q_BLOCK), to_query_blocks(c, Q_BLOCK),
                              jnp.arange(s // Q_BLOCK)))
    return from_query_blocks(out)


def dilated_window_attention(q, k, v):
    b, h, s, dh = q.shape
    scale = dh ** -0.5

    def one_block(args):
        q_blk, i = args
        qpos = i * Q_BLOCK + jnp.arange(Q_BLOCK)
        maxes, denoms, outs = [], [], []
        for window, dil in DILATED_BRANCHES:
            steps = jnp.arange(window // dil + 1)
            kpos = qpos[:, None] - dil * steps[None, :]
            idx = jnp.maximum(kpos, 0)
            k_g = jnp.take(k, idx, axis=2)
            v_g = jnp.take(v, idx, axis=2)
            logits = jnp.einsum('bhqd,bhqnd->bhqn', q_blk, k_g).astype(jnp.float32) * scale
            logits = jnp.where(kpos >= 0, logits, -jnp.inf)
            m = jnp.max(logits, axis=-1, keepdims=True)
            e = jnp.exp(logits - m)
            l = jnp.sum(e, axis=-1, keepdims=True)
            outs.append(jnp.einsum('bhqn,bhqnd->bhqd', e, v_g.astype(jnp.float32)) / l)
            maxes.append(m)
            denoms.append(l)
        m_all = jnp.stack(maxes)
        s_all = jnp.stack(denoms) * jnp.exp(m_all - jnp.max(m_all, axis=0, keepdims=True))
        alpha = s_all / jnp.sum(s_all, axis=0, keepdims=True)
        return jnp.sum(alpha * jnp.stack(outs), axis=0).astype(v.dtype)

    out = lax.map(one_block, (to_query_blocks(q, Q_BLOCK), jnp.arange(s // Q_BLOCK)))
    return from_query_blocks(out)


def mix_stick_moba(xn, w_in, g_q_b, g_k_b, cos, sin):
    q, k, v = jnp.split(xn @ w_in, 3, axis=-1)
    q, k, v = split_heads(q, N_HEADS), split_heads(k, N_HEADS), split_heads(v, N_HEADS)
    out_a = stick_breaking_attention(q[:, :H_A], k[:, :H_A], v[:, :H_A])
    q_b = apply_rope(rms_norm(q[:, H_A:], g_q_b), cos, sin)
    k_b = apply_rope(rms_norm(k[:, H_A:], g_k_b), cos, sin)
    out_b = moba_attention(q_b, k_b, v[:, H_A:])
    return merge_heads(jnp.concatenate([out_a.astype(xn.dtype), out_b.astype(xn.dtype)], axis=1))


def mix_forget_dilated(xn, w_in, b_f, g_q_c, g_k_c, g_q_d, g_k_d, cos, sin):
    proj = xn @ w_in
    q, k, v = jnp.split(proj[..., :3 * D_MIX], 3, axis=-1)
    q, k, v = split_heads(q, N_HEADS), split_heads(k, N_HEADS), split_heads(v, N_HEADS)
    log_f = jax.nn.log_sigmoid((proj[..., 3 * D_MIX:] + b_f).astype(jnp.float32)).transpose(0, 2, 1)
    q_c = rms_norm(q[:, :H_C], g_q_c)
    k_c = rms_norm(k[:, :H_C], g_k_c)
    out_c = forgetting_attention(q_c, k_c, v[:, :H_C], log_f)
    q_d = apply_rope(rms_norm(q[:, H_C:], g_q_d), cos, sin)
    k_d = apply_rope(rms_norm(k[:, H_C:], g_k_d), cos, sin)
    out_d = dilated_window_attention(q_d, k_d, v[:, H_C:])
    return merge_heads(jnp.concatenate([out_c.astype(xn.dtype), out_d.astype(xn.dtype)], axis=1))


def setup_inputs(seed: int = 0) -> dict:
    key = jax.random.key(seed)
    ks = jax.random.split(key, 20)

    def normal(k, shape, scale):
        return jax.random.normal(k, shape, jnp.float32) * scale

    def gain(k, shape):
        return 1.0 + 0.02 * jax.random.normal(k, shape, jnp.float32)

    return {
        'x': normal(ks[0], (BATCH, SEQ, D_MODEL), 1.0),
        'norm_ffn1': gain(ks[1], (DEPTH, D_MODEL)),
        'ffn1_w_gate': normal(ks[2], (DEPTH, D_MODEL, D_FF), D_MODEL ** -0.5),
        'ffn1_w_up': normal(ks[3], (DEPTH, D_MODEL, D_FF), D_MODEL ** -0.5),
        'ffn1_w_down': normal(ks[4], (DEPTH, D_FF, D_MODEL), D_FF ** -0.5),
        'norm_mix': gain(ks[5], (DEPTH, D_MODEL)),
        'w_in_ab': normal(ks[6], (N_EVEN, D_MODEL, 3 * D_MIX), D_MODEL ** -0.5),
        'g_q_b': gain(ks[7], (N_EVEN, HEAD_DIM)),
        'g_k_b': gain(ks[8], (N_EVEN, HEAD_DIM)),
        'w_in_cd': normal(ks[9], (N_ODD, D_MODEL, 3 * D_MIX + H_C), D_MODEL ** -0.5),
        'b_f': normal(ks[10], (N_ODD, H_C), 0.1),
        'g_q_c': gain(ks[11], (N_ODD, HEAD_DIM)),
        'g_k_c': gain(ks[12], (N_ODD, HEAD_DIM)),
        'g_q_d': gain(ks[13], (N_ODD, HEAD_DIM)),
        'g_k_d': gain(ks[14], (N_ODD, HEAD_DIM)),
        'w_out': normal(ks[15], (DEPTH, D_MIX, D_MODEL), D_MIX ** -0.5),
        'norm_ffn2': gain(ks[16], (DEPTH, D_MODEL)),
        'ffn2_w_gate': normal(ks[17], (DEPTH, D_MODEL, D_FF), D_MODEL ** -0.5),
        'ffn2_w_up': normal(ks[18], (DEPTH, D_MODEL, D_FF), D_MODEL ** -0.5),
        'ffn2_w_down': normal(ks[19], (DEPTH, D_FF, D_MODEL), D_FF ** -0.5),
    }


def reference(x, norm_ffn1, ffn1_w_gate, ffn1_w_up, ffn1_w_down, norm_mix, w_in_ab, g_q_b, g_k_b,
              w_in_cd, b_f, g_q_c, g_k_c, g_q_d, g_k_d, w_out, norm_ffn2, ffn2_w_gate, ffn2_w_up,
              ffn2_w_down):
    cos, sin = rope_tables(x.shape[1])
    for layer in range(DEPTH):
        j = layer // 2
        x = x + FFN_RES_WEIGHT * swiglu(rms_norm(x, norm_ffn1[layer]), ffn1_w_gate[layer],
                                        ffn1_w_up[layer], ffn1_w_down[layer])
        xn = rms_norm(x, norm_mix[layer])
        if layer % 2 == 0:
            mixed = mix_stick_moba(xn, w_in_ab[j], g_q_b[j], g_k_b[j], cos, sin)
        else:
            mixed = mix_forget_dilated(xn, w_in_cd[j], b_f[j], g_q_c[j], g_k_c[j],
                                       g_q_d[j], g_k_d[j], cos, sin)
        x = x + mixed @ w_out[layer]
        x = x + FFN_RES_WEIGHT * swiglu(rms_norm(x, norm_ffn2[layer]), ffn2_w_gate[layer],
                                        ffn2_w_up[layer], ffn2_w_down[layer])
    return x
```

```python
import functools

import numpy as np
import jax
import jax.numpy as jnp
from jax import lax
from jax.experimental import pallas as pl
from jax.experimental.pallas import tpu as pltpu

F32 = jnp.float32
BF16 = jnp.bfloat16

D_MODEL = 1024
HEAD_DIM = 64
D_MIX = 1024
D_FF = 2816
GROUP_HEADS = 8
GROUP_WIDTH = GROUP_HEADS * HEAD_DIM
LANES = 128
HEADS_PER_BLOCK = LANES // HEAD_DIM
N_PAIRS = GROUP_WIDTH // LANES
RMS_EPS = 1e-6
ROPE_THETA = 10000.0
FFN_RES_WEIGHT = 0.5
ATTN_SCALE = HEAD_DIM ** -0.5
MOBA_BLOCK = 256
MOBA_TOPK = 3
DILATED_BRANCHES = ((128, 1), (512, 4), (2048, 16))
SKIP_LOG = 88.0
VMEM_LIMIT_BYTES = 56 * 1024 * 1024

TOKEN_TILE = 512
FF_CHUNK = 256
PROJ_CHUNK = 256
SB_Q_TILE = 256
SB_K_TILE = LANES
ATTN_TILE = 256
CUMSUM_TILE = 256

_NT = (((1,), (1,)), ((), ()))


def _params(n_grid):
    return pltpu.CompilerParams(
        dimension_semantics=("arbitrary",) * n_grid,
        vmem_limit_bytes=VMEM_LIMIT_BYTES)


def _resident(shape):
    nd = len(shape)
    return pl.BlockSpec(shape, lambda *_: (0,) * nd, pipeline_mode=pl.Buffered(1))


def _dot(a, b):
    return jnp.dot(a, b, preferred_element_type=F32)


def _dot_nt(a, b):
    return lax.dot_general(a, b, _NT, preferred_element_type=F32)


def _split2(x):
    hi = x.astype(BF16)
    lo = (x - hi.astype(F32)).astype(BF16)
    return hi, lo


def _rms_rows(x, gain):
    return x * lax.rsqrt(jnp.mean(x * x, axis=-1, keepdims=True) + RMS_EPS) * gain


def _swiglu_residual(x, g_ref, wg_ref, wu_ref, wd_ref, h_ref):
    xn = _rms_rows(x, g_ref[...]).astype(BF16)
    for c in range(D_FF // FF_CHUNK):
        sl = slice(c * FF_CHUNK, (c + 1) * FF_CHUNK)
        gate = _dot(xn, wg_ref[:, sl])
        up = _dot(xn, wu_ref[:, sl])
        h_ref[:, sl] = (gate * jax.nn.sigmoid(gate) * up).astype(BF16)
    return x + FFN_RES_WEIGHT * _dot(h_ref[...], wd_ref[...])


def _ffn_kernel(x_ref, g_ref, wg_ref, wu_ref, wd_ref, o_ref, h_ref):
    o_ref[...] = _swiglu_residual(x_ref[...], g_ref, wg_ref, wu_ref, wd_ref, h_ref)


def _mix_ffn_kernel(x_ref, a_ref, b_ref, wo_ref, g_ref, wg_ref, wu_ref, wd_ref, o_ref, h_ref):
    x = (x_ref[...] + _dot(a_ref[...], wo_ref[0:GROUP_WIDTH, :])
         + _dot(b_ref[...], wo_ref[GROUP_WIDTH:D_MIX, :]))
    o_ref[...] = _swiglu_residual(x, g_ref, wg_ref, wu_ref, wd_ref, h_ref)


def _ffn_specs():
    return [pl.BlockSpec((1, D_MODEL), lambda i: (0, 0)),
            _resident((D_MODEL, D_FF)), _resident((D_MODEL, D_FF)), _resident((D_FF, D_MODEL))]


def _ffn(x2, gain, wg, wu, wd):
    n = x2.shape[0]
    row = pl.BlockSpec((TOKEN_TILE, D_MODEL), lambda i: (i, 0))
    return pl.pallas_call(
        _ffn_kernel,
        grid=(n // TOKEN_TILE,),
        in_specs=[row] + _ffn_specs(),
        out_specs=row,
        out_shape=jax.ShapeDtypeStruct((n, D_MODEL), F32),
        scratch_shapes=[pltpu.VMEM((TOKEN_TILE, D_FF), BF16)],
        compiler_params=_params(1),
        name="ffn",
    )(x2, gain, wg, wu, wd)


def _mix_ffn(x2, oa, ob, wo, gain, wg, wu, wd):
    n = x2.shape[0]
    row = pl.BlockSpec((TOKEN_TILE, D_MODEL), lambda i: (i, 0))
    half = pl.BlockSpec((TOKEN_TILE, GROUP_WIDTH), lambda i: (i, 0))
    return pl.pallas_call(
        _mix_ffn_kernel,
        grid=(n // TOKEN_TILE,),
        in_specs=[row, half, half, _resident((D_MIX, D_MODEL))] + _ffn_specs(),
        out_specs=row,
        out_shape=jax.ShapeDtypeStruct((n, D_MODEL), F32),
        scratch_shapes=[pltpu.VMEM((TOKEN_TILE, D_FF), BF16)],
        compiler_params=_params(1),
        name="mix_ffn",
    )(x2, oa, ob, wo, gain, wg, wu, wd)


def _head_sum_matrix():
    r = lax.broadcasted_iota(jnp.int32, (LANES, LANES), 0) // HEAD_DIM
    c = lax.broadcasted_iota(jnp.int32, (LANES, LANES), 1) // HEAD_DIM
    return jnp.where(r == c, 1.0, 0.0).astype(BF16)


def _head_rms(y, gain, hsum):
    hi, lo = _split2(y * y)
    ss = _dot(hi, hsum) + _dot(lo, hsum)
    return y * lax.rsqrt(ss * (1.0 / HEAD_DIM) + RMS_EPS) * gain


def _rope(y, cos, sin_signed, first_half):
    half = HEAD_DIM // 2
    partner = jnp.where(first_half, pltpu.roll(y, LANES - half, 1), pltpu.roll(y, half, 1))
    return y * cos + partner * sin_signed


def _proj_blocks(xn, w_ref, n_blocks):
    per = PROJ_CHUNK // LANES
    for c in range(0, n_blocks, per):
        width = min(per, n_blocks - c) * LANES
        y = _dot(xn, w_ref[:, c * LANES:c * LANES + width])
        for s in range(width // LANES):
            yield c + s, y[:, s * LANES:(s + 1) * LANES]


def _first_half_mask(rows):
    lane = lax.broadcasted_iota(jnp.int32, (rows, LANES), 1)
    return (lane % HEAD_DIM) < (HEAD_DIM // 2)


def _proj_ab_kernel(x_ref, g_ref, w_ref, gq_ref, gk_ref, cos_ref, sin_ref,
                    qkv_ref, qf_ref, km_ref):
    xn = _rms_rows(x_ref[...], g_ref[...]).astype(BF16)
    hsum = _head_sum_matrix()
    cos, sin = cos_ref[...], sin_ref[...]
    first_half = _first_half_mask(TOKEN_TILE)
    blocks_per_kind = D_MIX // LANES
    for cb, y in _proj_blocks(xn, w_ref, 3 * blocks_per_kind):
        kind, pp = divmod(cb, blocks_per_kind)
        if kind < 2 and pp >= N_PAIRS:
            gain = (gq_ref if kind == 0 else gk_ref)[...]
            y = _rope(_head_rms(y, gain, hsum), cos, sin, first_half)
            sl = slice((pp - N_PAIRS) * LANES, (pp - N_PAIRS + 1) * LANES)
            if kind == 0:
                qf_ref[:, sl] = y
            else:
                for jb in range(TOKEN_TILE // MOBA_BLOCK):
                    rows = y[jb * MOBA_BLOCK:(jb + 1) * MOBA_BLOCK, :]
                    km_ref[jb, :, sl] = jnp.mean(rows, axis=0, keepdims=True)
        if kind == 0:
            y = y * ATTN_SCALE
        qkv_ref[:, cb * LANES:(cb + 1) * LANES] = y.astype(BF16)


def _proj_cd_kernel(x_ref, g_ref, w_ref, gqc_ref, gkc_ref, gqd_ref, gkd_ref, bf_ref,
                    cos_ref, sin_ref, qkv_ref, lf_ref):
    xn = _rms_rows(x_ref[...], g_ref[...]).astype(BF16)
    hsum = _head_sum_matrix()
    cos, sin = cos_ref[...], sin_ref[...]
    first_half = _first_half_mask(TOKEN_TILE)
    blocks_per_kind = D_MIX // LANES
    for cb, y in _proj_blocks(xn, w_ref, 3 * blocks_per_kind + 1):
        kind, pp = divmod(cb, blocks_per_kind)
        if kind == 3:
            t = y + bf_ref[...]
            lf_ref[...] = jnp.minimum(t, 0.0) - jnp.log1p(jnp.exp(-jnp.abs(t)))
            continue
        if kind < 2:
            if pp < N_PAIRS:
                y = _head_rms(y, (gqc_ref if kind == 0 else gkc_ref)[...], hsum)
            else:
                y = _head_rms(y, (gqd_ref if kind == 0 else gkd_ref)[...], hsum)
                y = _rope(y, cos, sin, first_half)
        if kind == 0:
            y = y * ATTN_SCALE
        qkv_ref[:, cb * LANES:(cb + 1) * LANES] = y.astype(BF16)


def _proj_common_specs(seq, w_cols):
    row = pl.BlockSpec((TOKEN_TILE, D_MODEL), lambda i: (i, 0))
    tiles_per_seq = seq // TOKEN_TILE
    table = pl.BlockSpec((TOKEN_TILE, LANES), lambda i: (i % tiles_per_seq, 0))
    lane_vec = pl.BlockSpec((1, LANES), lambda i: (0, 0))
    gain = pl.BlockSpec((1, D_MODEL), lambda i: (0, 0))
    return row, gain, _resident((D_MODEL, w_cols)), lane_vec, table


def _proj_ab(x2, seq, gain, w, gq, gk, cos, sin):
    n = x2.shape[0]
    row, gspec, wspec, lane_vec, table = _proj_common_specs(seq, 3 * D_MIX)
    blocks = TOKEN_TILE // MOBA_BLOCK
    return pl.pallas_call(
        _proj_ab_kernel,
        grid=(n // TOKEN_TILE,),
        in_specs=[row, gspec, wspec, lane_vec, lane_vec, table, table],
        out_specs=[pl.BlockSpec((TOKEN_TILE, 3 * D_MIX), lambda i: (i, 0)),
                   pl.BlockSpec((TOKEN_TILE, GROUP_WIDTH), lambda i: (i, 0)),
                   pl.BlockSpec((blocks, 1, GROUP_WIDTH), lambda i: (i, 0, 0))],
        out_shape=[jax.ShapeDtypeStruct((n, 3 * D_MIX), BF16),
                   jax.ShapeDtypeStruct((n, GROUP_WIDTH), F32),
                   jax.ShapeDtypeStruct((n // MOBA_BLOCK, 1, GROUP_WIDTH), F32)],
        compiler_params=_params(1),
        name="proj_ab",
    )(x2, gain, w, gq, gk, cos, sin)


def _proj_cd(x2, seq, gain, w, gqc, gkc, gqd, gkd, bf, cos, sin):
    n = x2.shape[0]
    row, gspec, wspec, lane_vec, table = _proj_common_specs(seq, 3 * D_MIX + LANES)
    return pl.pallas_call(
        _proj_cd_kernel,
        grid=(n // TOKEN_TILE,),
        in_specs=[row, gspec, wspec, lane_vec, lane_vec, lane_vec, lane_vec, lane_vec, table, table],
        out_specs=[pl.BlockSpec((TOKEN_TILE, 3 * D_MIX), lambda i: (i, 0)),
                   pl.BlockSpec((TOKEN_TILE, LANES), lambda i: (i, 0))],
        out_shape=[jax.ShapeDtypeStruct((n, 3 * D_MIX), BF16),
                   jax.ShapeDtypeStruct((n, LANES), F32)],
        compiler_params=_params(1),
        name="proj_cd",
    )(x2, gain, w, gqc, gkc, gqd, gkd, bf, cos, sin)


def _cumsum_kernel(lf_ref, c_ref, carry_ref):
    @pl.when(pl.program_id(1) == 0)
    def _():
        carry_ref[...] = jnp.zeros_like(carry_ref)

    t = CUMSUM_TILE
    r = lax.broadcasted_iota(jnp.int32, (t, t), 0)
    c = lax.broadcasted_iota(jnp.int32, (t, t), 1)
    lower = jnp.where(c <= r, 1.0, 0.0).astype(BF16)
    lf = lf_ref[...]
    hi = lf.astype(BF16)
    mid, lo = _split2(lf - hi.astype(F32))
    out = _dot(lower, hi) + _dot(lower, mid) + _dot(lower, lo) + carry_ref[0:1, :]
    c_ref[...] = out
    carry_ref[0:1, :] = out[t - 1:t, :]


def _cumsum(lf3):
    b, s, _ = lf3.shape
    blk = pl.BlockSpec((None, CUMSUM_TILE, LANES), lambda bi, i: (bi, i, 0))
    return pl.pallas_call(
        _cumsum_kernel,
        grid=(b, s // CUMSUM_TILE),
        in_specs=[blk],
        out_specs=blk,
        out_shape=jax.ShapeDtypeStruct(lf3.shape, F32),
        scratch_shapes=[pltpu.VMEM((8, LANES), F32)],
        compiler_params=_params(2),
        name="forget_cumsum",
    )(lf3)


def _head_of_lane(rows):
    return lax.broadcasted_iota(jnp.int32, (rows, LANES), 1) // HEAD_DIM


def _softmax_step(s, m, l, acc, vb, weight=None):
    m_new = jnp.maximum(m, jnp.max(s, axis=1, keepdims=True))
    alpha = jnp.exp(m - m_new)
    p = jnp.exp(s - m_new)
    if weight is not None:
        p = p * weight
    l = alpha * l + jnp.sum(p, axis=1, keepdims=True)
    acc = alpha * acc + _dot(p.astype(BF16), vb)
    return m_new, l, acc


def _softmax_init(rows):
    return (jnp.full((rows, 1), -jnp.inf, F32), jnp.zeros((rows, 1), F32),
            jnp.zeros((rows, LANES), F32))


def _sb_kernel(q_ref, k_ref, v_ref, o_ref):
    tq, tk = SB_Q_TILE, SB_K_TILE
    i = pl.program_id(2)
    q = q_ref[...]
    head = _head_of_lane(tq)
    row = lax.broadcasted_iota(jnp.int32, (tq, tk), 0)
    col = lax.broadcasted_iota(jnp.int32, (tq, tk), 1)
    rr = lax.broadcasted_iota(jnp.int32, (2 * tk, tk + LANES), 0)
    rr = jnp.where(rr >= tk, rr - tk, rr)
    cc = lax.broadcasted_iota(jnp.int32, (2 * tk, tk + LANES), 1)
    sfx = jnp.where((cc >= tk) | (rr > cc), 1.0, 0.0).astype(BF16)
    j_diag = (i + 1) * (tq // tk) - 1
    zeros = jnp.zeros((tq, LANES), F32)

    accs = []
    for h in range(HEADS_PER_BLOCK):
        qh = jnp.where(head == h, q, jnp.zeros_like(q))

        def cond(carry):
            j, run, _ = carry
            return (j >= 0) & (jnp.max(run) > -SKIP_LOG)

        def body(carry, qh=qh):
            j, run, acc = carry
            start = pl.multiple_of(j * tk, tk)
            kb = k_ref[pl.ds(start, tk), :]
            vb = v_ref[pl.ds(start, tk), :]
            z = _dot_nt(qh, kb)
            strict = (start + col) < (i * tq + row)
            soft = jnp.log1p(jnp.exp(-jnp.abs(z)))
            log_beta = jnp.minimum(z, 0.0) - soft
            log_keep = jnp.where(strict, jnp.minimum(-z, 0.0) - soft, 0.0)
            hi, lo = _split2(log_keep)
            sums = _dot(jnp.concatenate([hi, lo], axis=1), sfx)
            w = jnp.where(strict, jnp.exp(log_beta + run + sums[:, :tk]), 0.0)
            acc = acc + _dot(w.astype(BF16), vb)
            return j - 1, run + sums[:, tk:], acc

        _, _, acc = lax.while_loop(cond, body, (j_diag, zeros, zeros))
        accs.append(acc)
    o_ref[...] = jnp.where(head == 0, accs[0], accs[1]).astype(o_ref.dtype)


def _moba_kernel(q_ref, qf_ref, km_ref, k_ref, v_ref, o_ref):
    t = MOBA_BLOCK
    own = pl.program_id(2)
    nkb = km_ref.shape[0]
    q, qf = q_ref[...], qf_ref[...]
    km_hi, km_lo = _split2(km_ref[...])
    head = _head_of_lane(t)
    row = lax.broadcasted_iota(jnp.int32, (t, t), 0)
    col = lax.broadcasted_iota(jnp.int32, (t, t), 1)
    blk = lax.broadcasted_iota(jnp.int32, (t, nkb), 1)
    blk_f = blk.astype(F32)
    own_start = pl.multiple_of(own * t, t)

    outs = []
    for h in range(HEADS_PER_BLOCK):
        qh = jnp.where(head == h, q, jnp.zeros_like(q))
        q_hi, q_lo = _split2(jnp.where(head == h, qf, 0.0))
        gate = _dot_nt(q_hi, km_hi) + _dot_nt(q_hi, km_lo) + _dot_nt(q_lo, km_hi)
        gate = jnp.where(blk < own, gate, -jnp.inf)
        sel = jnp.zeros((t, nkb), F32)
        for r in range(MOBA_TOPK):
            best = jnp.max(gate, axis=1, keepdims=True)
            first = jnp.min(jnp.where(gate == best, blk_f, float(nkb)), axis=1, keepdims=True)
            hit = blk_f == first
            sel = sel + jnp.where(hit, jnp.where(r < own, 1.0, 0.0), 0.0)
            gate = jnp.where(hit, -jnp.inf, gate)

        s = _dot_nt(qh, k_ref[pl.ds(own_start, t), :])
        s = jnp.where(col <= row, s, -jnp.inf)
        state = _softmax_step(s, *_softmax_init(t), v_ref[pl.ds(own_start, t), :])

        def body(j, state, qh=qh, sel=sel):
            picked = jnp.sum(jnp.where(blk == j, sel, 0.0), axis=1, keepdims=True) > 0.5
            start = pl.multiple_of(j * t, t)
            s = jnp.where(picked, _dot_nt(qh, k_ref[pl.ds(start, t), :]), -jnp.inf)
            return _softmax_step(s, *state, v_ref[pl.ds(start, t), :])

        _, l, acc = lax.fori_loop(0, own, body, state)
        outs.append(acc / l)
    o_ref[...] = jnp.where(head == 0, outs[0], outs[1]).astype(o_ref.dtype)


def _fox_kernel(q_ref, c_ref, ct_ref, k_ref, v_ref, o_ref, knorm_ref):
    t = ATTN_TILE
    pair = pl.program_id(1)
    i = pl.program_id(2)
    seq = k_ref.shape[0]
    hsum = _head_sum_matrix()

    @pl.when(i == 0)
    def _():
        def blk_max(j, best):
            kb = k_ref[pl.ds(pl.multiple_of(j * t, t), t), :].astype(F32)
            ss = _dot((kb * kb).astype(BF16), hsum)
            return jnp.maximum(best, jnp.max(ss, axis=0, keepdims=True))
        best = lax.fori_loop(0, seq // t, blk_max, jnp.zeros((1, LANES), F32))
        knorm_ref[...] = jnp.broadcast_to(best, knorm_ref.shape)

    q = q_ref[...]
    qf = q.astype(F32)
    qn2 = _dot((qf * qf).astype(BF16), hsum)
    qk_bound = 1.05 * jnp.sqrt(qn2 * knorm_ref[0:1, :]) + 1e-6
    c_blk = c_ref[...]
    lane = lax.broadcasted_iota(jnp.int32, (t, LANES), 1)
    head = lane // HEAD_DIM
    row = lax.broadcasted_iota(jnp.int32, (t, t), 0)
    col = lax.broadcasted_iota(jnp.int32, (t, t), 1)
    own_start = pl.multiple_of(i * t, t)

    outs = []
    for h in range(HEADS_PER_BLOCK):
        qh = jnp.where(head == h, q, jnp.zeros_like(q))
        c_q = jnp.sum(jnp.where(lane == HEADS_PER_BLOCK * pair + h, c_blk, 0.0),
                      axis=1, keepdims=True)
        reach = qk_bound[:, h * HEAD_DIM:h * HEAD_DIM + 1] + c_q

        s = _dot_nt(qh, k_ref[pl.ds(own_start, t), :]) + (c_q - ct_ref[h:h + 1, pl.ds(own_start, t)])
        s = jnp.where(col <= row, s, -jnp.inf)
        state = _softmax_step(s, *_softmax_init(t), v_ref[pl.ds(own_start, t), :])

        def cond(carry, reach=reach):
            j, m, _, _ = carry
            start = pl.multiple_of(jnp.maximum(j, 0) * t, t)
            c_k = ct_ref[h:h + 1, pl.ds(start, t)]
            return (j >= 0) & (jnp.max(reach - m) - jnp.min(c_k) > -SKIP_LOG)

        def body(carry, qh=qh, c_q=c_q):
            j, m, l, acc = carry
            start = pl.multiple_of(j * t, t)
            s = _dot_nt(qh, k_ref[pl.ds(start, t), :]) + (c_q - ct_ref[h:h + 1, pl.ds(start, t)])
            return (j - 1,) + _softmax_step(s, m, l, acc, v_ref[pl.ds(start, t), :])

        _, _, l, acc = lax.while_loop(cond, body, (i - 1,) + state)
        outs.append(acc / l)
    o_ref[...] = jnp.where(head == 0, outs[0], outs[1]).astype(o_ref.dtype)


def _dilated_kernel(q_ref, cnt_ref, k_ref, v_ref, o_ref):
    t = ATTN_TILE
    i = pl.program_id(2)
    q = q_ref[...]
    head = _head_of_lane(t)
    n_back = cnt_ref.shape[0] - 1
    own_start = pl.multiple_of(i * t, t)

    outs = []
    for h in range(HEADS_PER_BLOCK):
        qh = jnp.where(head == h, q, jnp.zeros_like(q))

        def step(start, cnt, state, qh=qh):
            s = jnp.where(cnt > 0.0, _dot_nt(qh, k_ref[pl.ds(start, t), :]), -jnp.inf)
            return _softmax_step(s, *state, v_ref[pl.ds(start, t), :], weight=cnt)

        state = step(own_start, cnt_ref[0], _softmax_init(t))

        def body(d, state):
            start = pl.multiple_of((i - 1 - d) * t, t)
            return step(start, cnt_ref[d + 1], state)

        _, l, acc = lax.fori_loop(0, jnp.minimum(i, n_back), body, state)
        outs.append(acc / l)
    o_ref[...] = jnp.where(head == 0, outs[0], outs[1]).astype(o_ref.dtype)


def _qkv_specs(seq, tq, group):
    blocks_per_kind = D_MIX // LANES
    off = group * N_PAIRS
    q = pl.BlockSpec((None, tq, LANES), lambda b, p, i: (b, i, off + p))
    k = pl.BlockSpec((None, seq, LANES), lambda b, p, i: (b, 0, blocks_per_kind + off + p))
    v = pl.BlockSpec((None, seq, LANES), lambda b, p, i: (b, 0, 2 * blocks_per_kind + off + p))
    return q, k, v


def _attn_call(kernel_fn, name, qkv3, tq, group, extra_specs, extra_args, scratch=()):
    b, seq, _ = qkv3.shape
    q, k, v = _qkv_specs(seq, tq, group)
    return pl.pallas_call(
        kernel_fn,
        grid=(b, N_PAIRS, seq // tq),
        in_specs=[q] + list(extra_specs) + [k, v],
        out_specs=pl.BlockSpec((None, tq, LANES), lambda b, p, i: (b, i, p)),
        out_shape=jax.ShapeDtypeStruct((b, seq, GROUP_WIDTH), BF16),
        scratch_shapes=list(scratch),
        compiler_params=_params(3),
        name=name,
    )(qkv3, *extra_args, qkv3, qkv3)


def _dilated_counts(t, n_back):
    r = np.arange(t)[:, None]
    c = np.arange(t)[None, :]
    out = np.zeros((n_back + 1, t, t), np.float32)
    for d in range(n_back + 1):
        dist = d * t + r - c
        for window, dil in DILATED_BRANCHES:
            out[d] += (dist >= 0) & (dist <= window) & (dist % dil == 0)
    return out


def _rope_tables(seq):
    inv = 1.0 / (ROPE_THETA ** (jnp.arange(0, HEAD_DIM, 2, dtype=F32) / HEAD_DIM))
    ang = jnp.arange(seq, dtype=F32)[:, None] * inv[None, :]
    cos, sin = jnp.cos(ang), jnp.sin(ang)
    reps = LANES // HEAD_DIM
    cos_l = jnp.tile(jnp.concatenate([cos, cos], axis=1), (1, reps))
    sin_l = jnp.tile(jnp.concatenate([-sin, sin], axis=1), (1, reps))
    return cos_l, sin_l


def _lane_gain(g):
    return jnp.tile(g.astype(F32), LANES // HEAD_DIM)[None, :]


def kernel(x, norm_ffn1, ffn1_w_gate, ffn1_w_up, ffn1_w_down, norm_mix, w_in_ab, g_q_b, g_k_b,
           w_in_cd, b_f, g_q_c, g_k_c, g_q_d, g_k_d, w_out, norm_ffn2, ffn2_w_gate, ffn2_w_up,
           ffn2_w_down):
    b, seq, d = x.shape
    n = b * seq
    assert d == D_MODEL and n % TOKEN_TILE == 0 and seq % TOKEN_TILE == 0
    assert seq % MOBA_BLOCK == 0 and seq % SB_Q_TILE == 0 and seq % ATTN_TILE == 0
    cos, sin = _rope_tables(seq)
    bf = lambda w: w.astype(BF16)
    row = lambda g: g.astype(F32)[None, :]
    x2 = x.reshape(n, d)

    x2 = _ffn(x2, row(norm_ffn1[0]), bf(ffn1_w_gate[0]), bf(ffn1_w_up[0]), bf(ffn1_w_down[0]))
    qkv, qf, km = _proj_ab(x2, seq, row(norm_mix[0]), bf(w_in_ab[0]),
                           _lane_gain(g_q_b[0]), _lane_gain(g_k_b[0]), cos, sin)
    qkv3 = qkv.reshape(b, seq, 3 * D_MIX)
    nkb = seq // MOBA_BLOCK
    out_a = _attn_call(_sb_kernel, "stick_breaking", qkv3, SB_Q_TILE, 0, [], [])
    out_b = _attn_call(
        _moba_kernel, "moba", qkv3, MOBA_BLOCK, 1,
        [pl.BlockSpec((None, MOBA_BLOCK, LANES), lambda b, p, i: (b, i, p)),
         pl.BlockSpec((None, nkb, LANES), lambda b, p, i: (b, 0, p))],
        [qf.reshape(b, seq, GROUP_WIDTH), km.reshape(b, nkb, GROUP_WIDTH)])
    x2 = _mix_ffn(x2, out_a.reshape(n, GROUP_WIDTH), out_b.reshape(n, GROUP_WIDTH), bf(w_out[0]),
                  row(norm_ffn2[0]), bf(ffn2_w_gate[0]), bf(ffn2_w_up[0]), bf(ffn2_w_down[0]))

    x2 = _ffn(x2, row(norm_ffn1[1]), bf(ffn1_w_gate[1]), bf(ffn1_w_up[1]), bf(ffn1_w_down[1]))
    w_cd = jnp.pad(bf(w_in_cd[0]), ((0, 0), (0, LANES - GROUP_HEADS)))
    b_f_l = jnp.pad(b_f[0].astype(F32), (0, LANES - GROUP_HEADS))[None, :]
    qkv, lf = _proj_cd(x2, seq, row(norm_mix[1]), w_cd, _lane_gain(g_q_c[0]), _lane_gain(g_k_c[0]),
                       _lane_gain(g_q_d[0]), _lane_gain(g_k_d[0]), b_f_l, cos, sin)
    qkv3 = qkv.reshape(b, seq, 3 * D_MIX)
    c = _cumsum(lf.reshape(b, seq, LANES))
    c_t = jnp.swapaxes(c[:, :, :GROUP_HEADS], 1, 2).reshape(b, N_PAIRS, HEADS_PER_BLOCK, seq)
    out_c = _attn_call(
        _fox_kernel, "forgetting", qkv3, ATTN_TILE, 0,
        [pl.BlockSpec((None, ATTN_TILE, LANES), lambda b, p, i: (b, i, 0)),
         pl.BlockSpec((None, None, HEADS_PER_BLOCK, seq), lambda b, p, i: (b, p, 0, 0))],
        [c, c_t], scratch=[pltpu.VMEM((8, LANES), F32)])
    n_back = max(w for w, _ in DILATED_BRANCHES) // ATTN_TILE
    counts = jnp.asarray(_dilated_counts(ATTN_TILE, n_back))
    out_d = _attn_call(
        _dilated_kernel, "dilated", qkv3, ATTN_TILE, 1,
        [_resident((n_back + 1, ATTN_TILE, ATTN_TILE))], [counts])
    x2 = _mix_ffn(x2, out_c.reshape(n, GROUP_WIDTH), out_d.reshape(n, GROUP_WIDTH), bf(w_out[1]),
                  row(norm_ffn2[1]), bf(ffn2_w_gate[1]), bf(ffn2_w_up[1]), bf(ffn2_w_down[1]))
    return x2.reshape(b, seq, d)
```

```python
import functools

import numpy as np
import jax
import jax.numpy as jnp
from jax import lax
from jax.experimental import pallas as pl
from jax.experimental.pallas import tpu as pltpu

F32 = jnp.float32
BF16 = jnp.bfloat16

D_MODEL = 1024
HEAD_DIM = 64
D_MIX = 1024
D_FF = 2816
GROUP_HEADS = 8
GROUP_WIDTH = GROUP_HEADS * HEAD_DIM
LANES = 128
HEADS_PER_BLOCK = LANES // HEAD_DIM
N_PAIRS = GROUP_WIDTH // LANES
RMS_EPS = 1e-6
ROPE_THETA = 10000.0
FFN_RES_WEIGHT = 0.5
ATTN_SCALE = HEAD_DIM ** -0.5
MOBA_BLOCK = 256
MOBA_TOPK = 3
DILATED_BRANCHES = ((128, 1), (512, 4), (2048, 16))
SKIP_LOG = 88.0
VMEM_LIMIT_BYTES = 56 * 1024 * 1024

TOKEN_TILE = 512
FF_CHUNK = 256
PROJ_CHUNK = 256
SB_TILE = 256
ATTN_TILE = 256
MOBA_CHUNK = 2 * MOBA_BLOCK
DILATED_CHUNK = 2 * ATTN_TILE
MASK_BIG = 1e30
SHIFT_LIMIT = 40.0
CUMSUM_TILE = 256

_NT = (((1,), (1,)), ((), ()))


def _params(n_grid):
    return pltpu.CompilerParams(
        dimension_semantics=("arbitrary",) * n_grid,
        vmem_limit_bytes=VMEM_LIMIT_BYTES)


def _resident(shape):
    nd = len(shape)
    return pl.BlockSpec(shape, lambda *_: (0,) * nd, pipeline_mode=pl.Buffered(1))


def _dot(a, b):
    return jnp.dot(a, b, preferred_element_type=F32)


def _dot_nt(a, b):
    return lax.dot_general(a, b, _NT, preferred_element_type=F32)


def _split2(x):
    hi = x.astype(BF16)
    lo = (x - hi.astype(F32)).astype(BF16)
    return hi, lo


def _rms_rows(x, gain):
    return x * lax.rsqrt(jnp.mean(x * x, axis=-1, keepdims=True) + RMS_EPS) * gain


def _swiglu_residual(x, g_ref, wg_ref, wu_ref, wd_ref, h_ref):
    xn = _rms_rows(x, g_ref[...]).astype(BF16)
    for c in range(D_FF // FF_CHUNK):
        sl = slice(c * FF_CHUNK, (c + 1) * FF_CHUNK)
        gate = _dot(xn, wg_ref[:, sl])
        up = _dot(xn, wu_ref[:, sl])
        h_ref[:, sl] = (gate * jax.nn.sigmoid(gate) * up).astype(BF16)
    return x + FFN_RES_WEIGHT * _dot(h_ref[...], wd_ref[...])


def _ffn_kernel(x_ref, g_ref, wg_ref, wu_ref, wd_ref, o_ref, h_ref):
    o_ref[...] = _swiglu_residual(x_ref[...], g_ref, wg_ref, wu_ref, wd_ref, h_ref)


def _mix_ffn_kernel(x_ref, a_ref, b_ref, wo_ref, g_ref, wg_ref, wu_ref, wd_ref, o_ref, h_ref):
    x = (x_ref[...] + _dot(a_ref[...], wo_ref[0:GROUP_WIDTH, :])
         + _dot(b_ref[...], wo_ref[GROUP_WIDTH:D_MIX, :]))
    o_ref[...] = _swiglu_residual(x, g_ref, wg_ref, wu_ref, wd_ref, h_ref)


def _ffn_specs():
    return [pl.BlockSpec((1, D_MODEL), lambda i: (0, 0)),
            _resident((D_MODEL, D_FF)), _resident((D_MODEL, D_FF)), _resident((D_FF, D_MODEL))]


def _ffn(x2, gain, wg, wu, wd):
    n = x2.shape[0]
    row = pl.BlockSpec((TOKEN_TILE, D_MODEL), lambda i: (i, 0))
    return pl.pallas_call(
        _ffn_kernel,
        grid=(n // TOKEN_TILE,),
        in_specs=[row] + _ffn_specs(),
        out_specs=row,
        out_shape=jax.ShapeDtypeStruct((n, D_MODEL), F32),
        scratch_shapes=[pltpu.VMEM((TOKEN_TILE, D_FF), BF16)],
        compiler_params=_params(1),
        name="ffn",
    )(x2, gain, wg, wu, wd)


def _mix_ffn(x2, oa, ob, wo, gain, wg, wu, wd):
    n = x2.shape[0]
    row = pl.BlockSpec((TOKEN_TILE, D_MODEL), lambda i: (i, 0))
    half = pl.BlockSpec((TOKEN_TILE, GROUP_WIDTH), lambda i: (i, 0))
    return pl.pallas_call(
        _mix_ffn_kernel,
        grid=(n // TOKEN_TILE,),
        in_specs=[row, half, half, _resident((D_MIX, D_MODEL))] + _ffn_specs(),
        out_specs=row,
        out_shape=jax.ShapeDtypeStruct((n, D_MODEL), F32),
        scratch_shapes=[pltpu.VMEM((TOKEN_TILE, D_FF), BF16)],
        compiler_params=_params(1),
        name="mix_ffn",
    )(x2, oa, ob, wo, gain, wg, wu, wd)


def _head_sum_matrix():
    r = lax.broadcasted_iota(jnp.int32, (LANES, LANES), 0) // HEAD_DIM
    c = lax.broadcasted_iota(jnp.int32, (LANES, LANES), 1) // HEAD_DIM
    return jnp.where(r == c, 1.0, 0.0).astype(BF16)


def _head_rms(y, gain, hsum):
    hi, lo = _split2(y * y)
    ss = _dot(hi, hsum) + _dot(lo, hsum)
    return y * lax.rsqrt(ss * (1.0 / HEAD_DIM) + RMS_EPS) * gain


def _rope(y, cos, sin_signed, first_half):
    half = HEAD_DIM // 2
    partner = jnp.where(first_half, pltpu.roll(y, LANES - half, 1), pltpu.roll(y, half, 1))
    return y * cos + partner * sin_signed


def _proj_blocks(xn, w_ref, n_blocks):
    per = PROJ_CHUNK // LANES
    for c in range(0, n_blocks, per):
        width = min(per, n_blocks - c) * LANES
        y = _dot(xn, w_ref[:, c * LANES:c * LANES + width])
        for s in range(width // LANES):
            yield c + s, y[:, s * LANES:(s + 1) * LANES]


def _first_half_mask(rows):
    lane = lax.broadcasted_iota(jnp.int32, (rows, LANES), 1)
    return (lane % HEAD_DIM) < (HEAD_DIM // 2)


def _proj_ab_kernel(x_ref, g_ref, w_ref, gq_ref, gk_ref, cos_ref, sin_ref,
                    qkv_ref, qf_ref, km_ref):
    xn = _rms_rows(x_ref[...], g_ref[...]).astype(BF16)
    hsum = _head_sum_matrix()
    cos, sin = cos_ref[...], sin_ref[...]
    first_half = _first_half_mask(TOKEN_TILE)
    blocks_per_kind = D_MIX // LANES
    for cb, y in _proj_blocks(xn, w_ref, 3 * blocks_per_kind):
        kind, pp = divmod(cb, blocks_per_kind)
        if kind < 2 and pp >= N_PAIRS:
            gain = (gq_ref if kind == 0 else gk_ref)[...]
            y = _rope(_head_rms(y, gain, hsum), cos, sin, first_half)
            sl = slice((pp - N_PAIRS) * LANES, (pp - N_PAIRS + 1) * LANES)
            if kind == 0:
                qf_ref[:, sl] = y
            else:
                for jb in range(TOKEN_TILE // MOBA_BLOCK):
                    rows = y[jb * MOBA_BLOCK:(jb + 1) * MOBA_BLOCK, :]
                    km_ref[jb, :, sl] = jnp.mean(rows, axis=0, keepdims=True)
        if kind == 0:
            y = y * ATTN_SCALE
        qkv_ref[:, cb * LANES:(cb + 1) * LANES] = y.astype(BF16)


def _proj_cd_kernel(x_ref, g_ref, w_ref, gqc_ref, gkc_ref, gqd_ref, gkd_ref, bf_ref,
                    cos_ref, sin_ref, qkv_ref, lf_ref):
    xn = _rms_rows(x_ref[...], g_ref[...]).astype(BF16)
    hsum = _head_sum_matrix()
    cos, sin = cos_ref[...], sin_ref[...]
    first_half = _first_half_mask(TOKEN_TILE)
    blocks_per_kind = D_MIX // LANES
    for cb, y in _proj_blocks(xn, w_ref, 3 * blocks_per_kind + 1):
        kind, pp = divmod(cb, blocks_per_kind)
        if kind == 3:
            t = y + bf_ref[...]
            lf_ref[...] = jnp.minimum(t, 0.0) - jnp.log1p(jnp.exp(-jnp.abs(t)))
            continue
        if kind < 2:
            if pp < N_PAIRS:
                y = _head_rms(y, (gqc_ref if kind == 0 else gkc_ref)[...], hsum)
            else:
                y = _head_rms(y, (gqd_ref if kind == 0 else gkd_ref)[...], hsum)
                y = _rope(y, cos, sin, first_half)
        if kind == 0:
            y = y * ATTN_SCALE
        qkv_ref[:, cb * LANES:(cb + 1) * LANES] = y.astype(BF16)


def _proj_common_specs(seq, w_cols):
    row = pl.BlockSpec((TOKEN_TILE, D_MODEL), lambda i: (i, 0))
    tiles_per_seq = seq // TOKEN_TILE
    table = pl.BlockSpec((TOKEN_TILE, LANES), lambda i: (i % tiles_per_seq, 0))
    lane_vec = pl.BlockSpec((1, LANES), lambda i: (0, 0))
    gain = pl.BlockSpec((1, D_MODEL), lambda i: (0, 0))
    return row, gain, _resident((D_MODEL, w_cols)), lane_vec, table


def _proj_ab(x2, seq, gain, w, gq, gk, cos, sin):
    n = x2.shape[0]
    row, gspec, wspec, lane_vec, table = _proj_common_specs(seq, 3 * D_MIX)
    blocks = TOKEN_TILE // MOBA_BLOCK
    return pl.pallas_call(
        _proj_ab_kernel,
        grid=(n // TOKEN_TILE,),
        in_specs=[row, gspec, wspec, lane_vec, lane_vec, table, table],
        out_specs=[pl.BlockSpec((TOKEN_TILE, 3 * D_MIX), lambda i: (i, 0)),
                   pl.BlockSpec((TOKEN_TILE, GROUP_WIDTH), lambda i: (i, 0)),
                   pl.BlockSpec((blocks, 1, GROUP_WIDTH), lambda i: (i, 0, 0))],
        out_shape=[jax.ShapeDtypeStruct((n, 3 * D_MIX), BF16),
                   jax.ShapeDtypeStruct((n, GROUP_WIDTH), F32),
                   jax.ShapeDtypeStruct((n // MOBA_BLOCK, 1, GROUP_WIDTH), F32)],
        compiler_params=_params(1),
        name="proj_ab",
    )(x2, gain, w, gq, gk, cos, sin)


def _proj_cd(x2, seq, gain, w, gqc, gkc, gqd, gkd, bf, cos, sin):
    n = x2.shape[0]
    row, gspec, wspec, lane_vec, table = _proj_common_specs(seq, 3 * D_MIX + LANES)
    return pl.pallas_call(
        _proj_cd_kernel,
        grid=(n // TOKEN_TILE,),
        in_specs=[row, gspec, wspec, lane_vec, lane_vec, lane_vec, lane_vec, lane_vec, table, table],
        out_specs=[pl.BlockSpec((TOKEN_TILE, 3 * D_MIX), lambda i: (i, 0)),
                   pl.BlockSpec((TOKEN_TILE, LANES), lambda i: (i, 0))],
        out_shape=[jax.ShapeDtypeStruct((n, 3 * D_MIX), BF16),
                   jax.ShapeDtypeStruct((n, LANES), F32)],
        compiler_params=_params(1),
        name="proj_cd",
    )(x2, gain, w, gqc, gkc, gqd, gkd, bf, cos, sin)


def _cumsum_kernel(lf_ref, c_ref, carry_ref):
    @pl.when(pl.program_id(1) == 0)
    def _():
        carry_ref[...] = jnp.zeros_like(carry_ref)

    t = CUMSUM_TILE
    r = lax.broadcasted_iota(jnp.int32, (t, t), 0)
    c = lax.broadcasted_iota(jnp.int32, (t, t), 1)
    lower = jnp.where(c <= r, 1.0, 0.0).astype(BF16)
    lf = lf_ref[...]
    hi = lf.astype(BF16)
    mid, lo = _split2(lf - hi.astype(F32))
    out = _dot(lower, hi) + _dot(lower, mid) + _dot(lower, lo) + carry_ref[0:1, :]
    c_ref[...] = out
    carry_ref[0:1, :] = out[t - 1:t, :]


def _cumsum(lf3):
    b, s, _ = lf3.shape
    blk = pl.BlockSpec((None, CUMSUM_TILE, LANES), lambda bi, i: (bi, i, 0))
    return pl.pallas_call(
        _cumsum_kernel,
        grid=(b, s // CUMSUM_TILE),
        in_specs=[blk],
        out_specs=blk,
        out_shape=jax.ShapeDtypeStruct(lf3.shape, F32),
        scratch_shapes=[pltpu.VMEM((8, LANES), F32)],
        compiler_params=_params(2),
        name="forget_cumsum",
    )(lf3)


def _pair_queries(q, rows):
    head = lax.broadcasted_iota(jnp.int32, (rows, LANES), 1) // HEAD_DIM
    return head, [jnp.where(head == h, q, jnp.zeros_like(q)) for h in range(HEADS_PER_BLOCK)]


def _softmax_step(s, m, l, acc, vb):
    m_new = jnp.maximum(m, jnp.max(s, axis=1, keepdims=True))
    alpha = jnp.exp(m - m_new)
    p = jnp.exp(s - m_new)
    l = alpha * l + jnp.sum(p, axis=1, keepdims=True)
    acc = alpha * acc + _dot(p.astype(BF16), vb)
    return m_new, l, acc


def _softmax_init(rows):
    return (jnp.full((rows, 1), -jnp.inf, F32), jnp.zeros((rows, 1), F32),
            jnp.zeros((rows, LANES), F32))


def _loop_by_two(n, body, init):
    pairs = n // 2
    state = lax.fori_loop(0, pairs, lambda c, st: body(2 * c + 1, body(2 * c, st)), init)
    return lax.fori_loop(2 * pairs, n, body, state)


def _causal_bias(t):
    row = lax.broadcasted_iota(jnp.int32, (t, t), 0)
    col = lax.broadcasted_iota(jnp.int32, (t, t), 1)
    return jnp.where(col <= row, 0.0, -MASK_BIG)


def _store_heads(o_ref, head, outs):
    o_ref[...] = jnp.where(head == 0, outs[0], outs[1]).astype(o_ref.dtype)


def _sb_kernel(q_ref, k_ref, v_ref, o_ref):
    t = SB_TILE
    i = pl.program_id(2)
    head, qhs = _pair_queries(q_ref[...], t)
    row = lax.broadcasted_iota(jnp.int32, (t, t), 0)
    col = lax.broadcasted_iota(jnp.int32, (t, t), 1)
    rr = lax.broadcasted_iota(jnp.int32, (2 * t, t + LANES), 0)
    rr = jnp.where(rr >= t, rr - t, rr)
    cc = lax.broadcasted_iota(jnp.int32, (2 * t, t + LANES), 1)
    sfx = jnp.where((cc >= t) | (rr > cc), 1.0, 0.0).astype(BF16)

    def step(j, strict, run, acc, qh):
        start = pl.multiple_of(j * t, t)
        z = _dot_nt(qh, k_ref[pl.ds(start, t), :])
        soft = jnp.log1p(jnp.exp(-jnp.abs(z)))
        log_beta = jnp.minimum(z, 0.0) - soft
        log_keep = jnp.minimum(-z, 0.0) - soft
        if strict is not None:
            log_keep = jnp.where(strict, log_keep, 0.0)
        hi, lo = _split2(log_keep)
        sums = _dot(jnp.concatenate([hi, lo], axis=1), sfx)
        between = jnp.concatenate([run] * (t // LANES), axis=1) + sums[:, :t]
        w = jnp.exp(log_beta + between)
        if strict is not None:
            w = jnp.where(strict, w, 0.0)
        return run + sums[:, t:], acc + _dot(w.astype(BF16), v_ref[pl.ds(start, t), :])

    zeros = jnp.zeros((t, LANES), F32)
    carry = (i - 1,)
    for qh in qhs:
        carry += step(i, col < row, zeros, zeros, qh)

    def cond(carry):
        return (carry[0] >= 0) & (jnp.maximum(jnp.max(carry[1]), jnp.max(carry[3])) > -SKIP_LOG)

    def body(carry):
        out = (carry[0] - 1,)
        for h, qh in enumerate(qhs):
            out += step(carry[0], None, carry[1 + 2 * h], carry[2 + 2 * h], qh)
        return out

    carry = lax.while_loop(cond, body, carry)
    _store_heads(o_ref, head, [carry[2], carry[4]])


def _score_bound(q, k_ref, knorm_ref, first_step, t):
    hsum = _head_sum_matrix()

    @pl.when(first_step)
    def _():
        def blk_max(j, best):
            kb = k_ref[pl.ds(pl.multiple_of(j * t, t), t), :].astype(F32)
            ss = _dot((kb * kb).astype(BF16), hsum)
            return jnp.maximum(best, jnp.max(ss, axis=0, keepdims=True))
        best = lax.fori_loop(0, k_ref.shape[0] // t, blk_max, jnp.zeros((1, LANES), F32))
        knorm_ref[...] = jnp.broadcast_to(best, knorm_ref.shape)

    qf = q.astype(F32)
    qn2 = _dot((qf * qf).astype(BF16), hsum)
    return 1.05 * jnp.sqrt(qn2 * knorm_ref[0:1, :]) + 1e-6


def _moba_kernel(q_ref, qf_ref, km_ref, onehot_ref, k_ref, v_ref, o_ref, knorm_ref):
    t, ch = MOBA_BLOCK, MOBA_CHUNK
    per = ch // t
    own = pl.program_id(2)
    nkb = km_ref.shape[0]
    q, qf = q_ref[...], qf_ref[...]
    head, qhs = _pair_queries(q, t)
    bound = _score_bound(q, k_ref, knorm_ref, own == 0, t)
    lane = lax.broadcasted_iota(jnp.int32, (t, LANES), 1)
    km_hi, km_lo = _split2(km_ref[...])
    blk_t = lax.broadcasted_iota(jnp.int32, (nkb, t), 0)
    blk_tf = blk_t.astype(F32)
    causal = _causal_bias(t)
    own_start = pl.multiple_of(own * t, t)
    k_own = k_ref[pl.ds(own_start, t), :]
    v_own = v_ref[pl.ds(own_start, t), :]
    one_own = onehot_ref[pl.ds(own_start, t), :]
    n_chunks = (own + per - 1) // per

    blocked = []
    for h in range(HEADS_PER_BLOCK):
        q_hi, q_lo = _split2(jnp.where(head == h, qf, 0.0))
        gate = _dot_nt(km_hi, q_hi) + _dot_nt(km_lo, q_hi) + _dot_nt(km_hi, q_lo)
        gate = jnp.where(blk_t < own, gate, -jnp.inf)
        sel = jnp.zeros((nkb, t), F32)
        for r in range(MOBA_TOPK):
            best = jnp.max(gate, axis=0, keepdims=True)
            first = jnp.min(jnp.where(gate == best, blk_tf, float(nkb)), axis=0, keepdims=True)
            hit = blk_tf == first
            sel = sel + jnp.where(hit, jnp.where(r < own, 1.0, 0.0), 0.0)
            gate = jnp.where(hit, -jnp.inf, gate)
        neg = jnp.concatenate([(sel - 1.0) * MASK_BIG, jnp.zeros((LANES - nkb, t), F32)], axis=0)
        blocked.append(neg.T)

    def rhs_of(start, rows):
        return jnp.concatenate([k_ref[pl.ds(start, rows), :], onehot_ref[pl.ds(start, rows), :]], axis=1)

    def shifted():
        lhs, lhs_own = [], []
        for h, qh in enumerate(qhs):
            shift = bound[:, h * HEAD_DIM:h * HEAD_DIM + 1]
            extra = jnp.where(lane == LANES - 1, -shift, blocked[h])
            lhs.append(jnp.concatenate([qh, extra.astype(BF16)], axis=1))
            extra_own = jnp.where(lane == LANES - 1, -shift, 0.0)
            lhs_own.append(jnp.concatenate([qh, extra_own.astype(BF16)], axis=1))

        def accumulate(s, l, acc, vb):
            p = jnp.exp(s)
            for c0 in range(0, p.shape[1], LANES):
                l = l + p[:, c0:c0 + LANES]
            return l, acc + _dot(p.astype(BF16), vb)

        zeros = jnp.zeros((t, LANES), F32)
        rhs_own = jnp.concatenate([k_own, one_own], axis=1)
        states = tuple(accumulate(_dot_nt(a, rhs_own) + causal, zeros, zeros, v_own) for a in lhs_own)

        def body(c, states):
            start = pl.multiple_of(c * ch, ch)
            rhs = rhs_of(start, ch)
            vb = v_ref[pl.ds(start, ch), :]
            return tuple(accumulate(_dot_nt(a, rhs), l, acc, vb) for a, (l, acc) in zip(lhs, states))

        states = _loop_by_two(n_chunks, body, states)
        outs = [acc / jnp.sum(l, axis=1, keepdims=True) for l, acc in states]
        return jnp.where(head == 0, outs[0], outs[1])

    def running_max():
        lhs = [jnp.concatenate([qh, neg.astype(BF16)], axis=1) for qh, neg in zip(qhs, blocked)]
        states = tuple(_softmax_step(_dot_nt(qh, k_own) + causal, *_softmax_init(t), v_own)
                       for qh in qhs)

        def body(c, states):
            start = pl.multiple_of(c * ch, ch)
            rhs = rhs_of(start, ch)
            vb = v_ref[pl.ds(start, ch), :]
            return tuple(_softmax_step(_dot_nt(a, rhs), *st, vb) for a, st in zip(lhs, states))

        states = lax.fori_loop(0, n_chunks, body, states)
        outs = [acc / l for _, l, acc in states]
        return jnp.where(head == 0, outs[0], outs[1])

    out = lax.cond(jnp.max(bound) <= SHIFT_LIMIT, shifted, running_max)
    o_ref[...] = out.astype(o_ref.dtype)


def _fox_kernel(q_ref, c_ref, ct_ref, k_ref, v_ref, o_ref, knorm_ref):
    t = ATTN_TILE
    pair = pl.program_id(1)
    i = pl.program_id(2)

    q = q_ref[...]
    head, qhs = _pair_queries(q, t)
    qk_bound = _score_bound(q, k_ref, knorm_ref, i == 0, t)
    c_blk = c_ref[...]
    lane = lax.broadcasted_iota(jnp.int32, (t, LANES), 1)
    c_qs = [jnp.sum(jnp.where(lane == HEADS_PER_BLOCK * pair + h, c_blk, 0.0), axis=1, keepdims=True)
            for h in range(HEADS_PER_BLOCK)]
    reach = [qk_bound[:, h * HEAD_DIM:h * HEAD_DIM + 1] + c_qs[h] for h in range(HEADS_PER_BLOCK)]

    def step(j, states, extra):
        start = pl.multiple_of(j * t, t)
        kb = k_ref[pl.ds(start, t), :]
        vb = v_ref[pl.ds(start, t), :]
        out = ()
        for h, qh in enumerate(qhs):
            s = _dot_nt(qh, kb) + (c_qs[h] - ct_ref[h:h + 1, pl.ds(start, t)])
            if extra is not None:
                s = s + extra
            out += _softmax_step(s, *states[3 * h:3 * h + 3], vb)
        return out

    carry = (i - 1,) + step(i, _softmax_init(t) * HEADS_PER_BLOCK, _causal_bias(t))

    def cond(carry):
        j = carry[0]
        start = pl.multiple_of(jnp.maximum(j, 0) * t, t)
        margin = [jnp.max(reach[h] - carry[1 + 3 * h]) - jnp.min(ct_ref[h:h + 1, pl.ds(start, t)])
                  for h in range(HEADS_PER_BLOCK)]
        return (j >= 0) & (jnp.maximum(margin[0], margin[1]) > -SKIP_LOG)

    def body(carry):
        return (carry[0] - 1,) + step(carry[0], carry[1:], None)

    carry = lax.while_loop(cond, body, carry)
    _store_heads(o_ref, head, [carry[3 + 3 * h] / carry[2 + 3 * h] for h in range(HEADS_PER_BLOCK)])


def _dilated_kernel(q_ref, tbl_ref, k_ref, v_ref, o_ref, knorm_ref):
    t, ch = ATTN_TILE, DILATED_CHUNK
    i = pl.program_id(2)
    q = q_ref[...]
    head, qhs = _pair_queries(q, t)
    bound = _score_bound(q, k_ref, knorm_ref, i == 0, t)
    half = i // 2
    parity = i - 2 * half
    n_chunks = jnp.minimum(half + 1, tbl_ref.shape[0] // 2)

    def chunk(c):
        start = pl.multiple_of((half - c) * ch, ch)
        return k_ref[pl.ds(start, ch), :], v_ref[pl.ds(start, ch), :], tbl_ref[parity + 2 * c]

    def shifted():
        lane = lax.broadcasted_iota(jnp.int32, (t, LANES), 1)
        lhs = [jnp.concatenate(
            [qh, jnp.where(lane == LANES - 1, -bound[:, h * HEAD_DIM:h * HEAD_DIM + 1], 0.0).astype(BF16)],
            axis=1) for h, qh in enumerate(qhs)]
        ones_col = jnp.where(lax.broadcasted_iota(jnp.int32, (ch, LANES), 1) == LANES - 1, 1.0, 0.0).astype(BF16)

        def body(c, states):
            kb, vb, bias = chunk(c)
            rhs = jnp.concatenate([kb, ones_col], axis=1)
            out = ()
            for a, (l, acc) in zip(lhs, states):
                p = jnp.exp(_dot_nt(a, rhs) + bias)
                for c0 in range(0, ch, LANES):
                    l = l + p[:, c0:c0 + LANES]
                out += ((l, acc + _dot(p.astype(BF16), vb)),)
            return out

        zeros = jnp.zeros((t, LANES), F32)
        states = _loop_by_two(n_chunks, body, ((zeros, zeros),) * HEADS_PER_BLOCK)
        outs = [acc / jnp.sum(l, axis=1, keepdims=True) for l, acc in states]
        return jnp.where(head == 0, outs[0], outs[1])

    def running_max():
        def body(c, states):
            kb, vb, bias = chunk(c)
            return tuple(_softmax_step(_dot_nt(qh, kb) + bias, *st, vb) for qh, st in zip(qhs, states))

        states = lax.fori_loop(0, n_chunks, body, (_softmax_init(t),) * HEADS_PER_BLOCK)
        outs = [acc / l for _, l, acc in states]
        return jnp.where(head == 0, outs[0], outs[1])

    out = lax.cond(jnp.max(bound) <= SHIFT_LIMIT, shifted, running_max)
    o_ref[...] = out.astype(o_ref.dtype)


def _qkv_specs(seq, tq, group):
    blocks_per_kind = D_MIX // LANES
    off = group * N_PAIRS
    q = pl.BlockSpec((None, tq, LANES), lambda b, p, i: (b, i, off + p))
    k = pl.BlockSpec((None, seq, LANES), lambda b, p, i: (b, 0, blocks_per_kind + off + p))
    v = pl.BlockSpec((None, seq, LANES), lambda b, p, i: (b, 0, 2 * blocks_per_kind + off + p))
    return q, k, v


def _attn_call(kernel_fn, name, qkv3, tq, group, extra_specs, extra_args, scratch=()):
    b, seq, _ = qkv3.shape
    q, k, v = _qkv_specs(seq, tq, group)
    return pl.pallas_call(
        kernel_fn,
        grid=(b, N_PAIRS, seq // tq),
        in_specs=[q] + list(extra_specs) + [k, v],
        out_specs=pl.BlockSpec((None, tq, LANES), lambda b, p, i: (b, i, p)),
        out_shape=jax.ShapeDtypeStruct((b, seq, GROUP_WIDTH), BF16),
        scratch_shapes=list(scratch),
        compiler_params=_params(3),
        name=name,
    )(qkv3, *extra_args, qkv3, qkv3)


def _dilated_bias_table(t, ch):
    reach = max(w for w, _ in DILATED_BRANCHES)
    n = 2 * ((reach // t + 1 + ch // t) // 2)
    r = np.arange(t)[:, None]
    c = np.arange(ch)[None, :]
    out = np.zeros((n, t, ch), np.float32)
    for e in range(n):
        dist = e * t + r - c
        count = np.zeros((t, ch), np.float32)
        for window, dil in DILATED_BRANCHES:
            count += (dist >= 0) & (dist <= window) & (dist % dil == 0)
        out[e] = np.where(count > 0, np.log(np.maximum(count, 1.0)), -MASK_BIG)
    return out


def _block_onehot(seq):
    assert seq // MOBA_BLOCK < LANES
    out = np.zeros((seq, LANES), np.float32)
    out[np.arange(seq), np.arange(seq) // MOBA_BLOCK] = 1.0
    out[:, LANES - 1] = 1.0
    return jnp.asarray(out, dtype=BF16)


def _rope_tables(seq):
    inv = 1.0 / (ROPE_THETA ** (jnp.arange(0, HEAD_DIM, 2, dtype=F32) / HEAD_DIM))
    ang = jnp.arange(seq, dtype=F32)[:, None] * inv[None, :]
    cos, sin = jnp.cos(ang), jnp.sin(ang)
    reps = LANES // HEAD_DIM
    cos_l = jnp.tile(jnp.concatenate([cos, cos], axis=1), (1, reps))
    sin_l = jnp.tile(jnp.concatenate([-sin, sin], axis=1), (1, reps))
    return cos_l, sin_l


def _lane_gain(g):
    return jnp.tile(g.astype(F32), LANES // HEAD_DIM)[None, :]


def kernel(x, norm_ffn1, ffn1_w_gate, ffn1_w_up, ffn1_w_down, norm_mix, w_in_ab, g_q_b, g_k_b,
           w_in_cd, b_f, g_q_c, g_k_c, g_q_d, g_k_d, w_out, norm_ffn2, ffn2_w_gate, ffn2_w_up,
           ffn2_w_down):
    b, seq, d = x.shape
    n = b * seq
    assert d == D_MODEL and n % TOKEN_TILE == 0 and seq % TOKEN_TILE == 0
    assert seq % MOBA_CHUNK == 0 and seq % SB_TILE == 0 and seq % DILATED_CHUNK == 0
    cos, sin = _rope_tables(seq)
    bf = lambda w: w.astype(BF16)
    row = lambda g: g.astype(F32)[None, :]
    x2 = x.reshape(n, d)

    x2 = _ffn(x2, row(norm_ffn1[0]), bf(ffn1_w_gate[0]), bf(ffn1_w_up[0]), bf(ffn1_w_down[0]))
    qkv, qf, km = _proj_ab(x2, seq, row(norm_mix[0]), bf(w_in_ab[0]),
                           _lane_gain(g_q_b[0]), _lane_gain(g_k_b[0]), cos, sin)
    qkv3 = qkv.reshape(b, seq, 3 * D_MIX)
    nkb = seq // MOBA_BLOCK
    key_norm = [pltpu.VMEM((8, LANES), F32)]
    out_a = _attn_call(_sb_kernel, "stick_breaking", qkv3, SB_TILE, 0, [], [])
    out_b = _attn_call(
        _moba_kernel, "moba", qkv3, MOBA_BLOCK, 1,
        [pl.BlockSpec((None, MOBA_BLOCK, LANES), lambda b, p, i: (b, i, p)),
         pl.BlockSpec((None, nkb, LANES), lambda b, p, i: (b, 0, p)),
         _resident((seq, LANES))],
        [qf.reshape(b, seq, GROUP_WIDTH), km.reshape(b, nkb, GROUP_WIDTH), _block_onehot(seq)],
        scratch=key_norm)
    x2 = _mix_ffn(x2, out_a.reshape(n, GROUP_WIDTH), out_b.reshape(n, GROUP_WIDTH), bf(w_out[0]),
                  row(norm_ffn2[0]), bf(ffn2_w_gate[0]), bf(ffn2_w_up[0]), bf(ffn2_w_down[0]))

    x2 = _ffn(x2, row(norm_ffn1[1]), bf(ffn1_w_gate[1]), bf(ffn1_w_up[1]), bf(ffn1_w_down[1]))
    w_cd = jnp.pad(bf(w_in_cd[0]), ((0, 0), (0, LANES - GROUP_HEADS)))
    b_f_l = jnp.pad(b_f[0].astype(F32), (0, LANES - GROUP_HEADS))[None, :]
    qkv, lf = _proj_cd(x2, seq, row(norm_mix[1]), w_cd, _lane_gain(g_q_c[0]), _lane_gain(g_k_c[0]),
                       _lane_gain(g_q_d[0]), _lane_gain(g_k_d[0]), b_f_l, cos, sin)
    qkv3 = qkv.reshape(b, seq, 3 * D_MIX)
    c = _cumsum(lf.reshape(b, seq, LANES))
    c_t = jnp.swapaxes(c[:, :, :GROUP_HEADS], 1, 2).reshape(b, N_PAIRS, HEADS_PER_BLOCK, seq)
    out_c = _attn_call(
        _fox_kernel, "forgetting", qkv3, ATTN_TILE, 0,
        [pl.BlockSpec((None, ATTN_TILE, LANES), lambda b, p, i: (b, i, 0)),
         pl.BlockSpec((None, None, HEADS_PER_BLOCK, seq), lambda b, p, i: (b, p, 0, 0))],
        [c, c_t], scratch=key_norm)
    table = jnp.asarray(_dilated_bias_table(ATTN_TILE, DILATED_CHUNK))
    out_d = _attn_call(_dilated_kernel, "dilated", qkv3, ATTN_TILE, 1, [_resident(table.shape)], [table],
                       scratch=key_norm)
    x2 = _mix_ffn(x2, out_c.reshape(n, GROUP_WIDTH), out_d.reshape(n, GROUP_WIDTH), bf(w_out[1]),
                  row(norm_ffn2[1]), bf(ffn2_w_gate[1]), bf(ffn2_w_up[1]), bf(ffn2_w_down[1]))
    return x2.reshape(b, seq, d)
```

```python
import functools

import numpy as np
import jax
import jax.numpy as jnp
from jax import lax
from jax.experimental import pallas as pl
from jax.experimental.pallas import tpu as pltpu

F32 = jnp.float32
BF16 = jnp.bfloat16

D_MODEL = 1024
HEAD_DIM = 64
D_MIX = 1024
D_FF = 2816
GROUP_HEADS = 8
GROUP_WIDTH = GROUP_HEADS * HEAD_DIM
LANES = 128
HEADS_PER_BLOCK = LANES // HEAD_DIM
N_PAIRS = GROUP_WIDTH // LANES
RMS_EPS = 1e-6
ROPE_THETA = 10000.0
FFN_RES_WEIGHT = 0.5
ATTN_SCALE = HEAD_DIM ** -0.5
MOBA_BLOCK = 256
MOBA_TOPK = 3
DILATED_BRANCHES = ((128, 1), (512, 4), (2048, 16))
SKIP_LOG = 88.0
VMEM_LIMIT_BYTES = 56 * 1024 * 1024

TOKEN_TILE = 512
FF_CHUNK = 256
PROJ_CHUNK = 256
SB_TILE = 256
ATTN_TILE = 256
MOBA_CHUNK = 2 * MOBA_BLOCK
DILATED_CHUNK = 2 * ATTN_TILE
MASK_BIG = 1e30
SHIFT_LIMIT = 40.0
CUMSUM_TILE = 256

_NT = (((1,), (1,)), ((), ()))


def _params(n_grid):
    return pltpu.CompilerParams(
        dimension_semantics=("arbitrary",) * n_grid,
        vmem_limit_bytes=VMEM_LIMIT_BYTES)


def _resident(shape):
    nd = len(shape)
    return pl.BlockSpec(shape, lambda *_: (0,) * nd, pipeline_mode=pl.Buffered(1))


def _dot(a, b):
    return jnp.dot(a, b, preferred_element_type=F32)


def _dot_nt(a, b):
    return lax.dot_general(a, b, _NT, preferred_element_type=F32)


def _split2(x):
    hi = x.astype(BF16)
    lo = (x - hi.astype(F32)).astype(BF16)
    return hi, lo


def _rms_rows(x, gain):
    return x * lax.rsqrt(jnp.mean(x * x, axis=-1, keepdims=True) + RMS_EPS) * gain


def _swiglu_residual(x, g_ref, wg_ref, wu_ref, wd_ref, h_ref):
    xn = _rms_rows(x, g_ref[...]).astype(BF16)
    for c in range(D_FF // FF_CHUNK):
        sl = slice(c * FF_CHUNK, (c + 1) * FF_CHUNK)
        gate = _dot(xn, wg_ref[:, sl])
        up = _dot(xn, wu_ref[:, sl])
        h_ref[:, sl] = (gate * jax.nn.sigmoid(gate) * up).astype(BF16)
    return x + FFN_RES_WEIGHT * _dot(h_ref[...], wd_ref[...])


def _ffn_kernel(x_ref, g_ref, wg_ref, wu_ref, wd_ref, o_ref, h_ref):
    o_ref[...] = _swiglu_residual(x_ref[...], g_ref, wg_ref, wu_ref, wd_ref, h_ref)


def _mix_ffn_kernel(x_ref, a_ref, b_ref, wo_ref, g_ref, wg_ref, wu_ref, wd_ref, o_ref, h_ref):
    x = (x_ref[...] + _dot(a_ref[...], wo_ref[0:GROUP_WIDTH, :])
         + _dot(b_ref[...], wo_ref[GROUP_WIDTH:D_MIX, :]))
    o_ref[...] = _swiglu_residual(x, g_ref, wg_ref, wu_ref, wd_ref, h_ref)


def _ffn_specs():
    return [pl.BlockSpec((1, D_MODEL), lambda i: (0, 0)),
            _resident((D_MODEL, D_FF)), _resident((D_MODEL, D_FF)), _resident((D_FF, D_MODEL))]


def _ffn(x2, gain, wg, wu, wd):
    n = x2.shape[0]
    row = pl.BlockSpec((TOKEN_TILE, D_MODEL), lambda i: (i, 0))
    return pl.pallas_call(
        _ffn_kernel,
        grid=(n // TOKEN_TILE,),
        in_specs=[row] + _ffn_specs(),
        out_specs=row,
        out_shape=jax.ShapeDtypeStruct((n, D_MODEL), F32),
        scratch_shapes=[pltpu.VMEM((TOKEN_TILE, D_FF), BF16)],
        compiler_params=_params(1),
        name="ffn",
    )(x2, gain, wg, wu, wd)


def _mix_ffn(x2, oa, ob, wo, gain, wg, wu, wd):
    n = x2.shape[0]
    row = pl.BlockSpec((TOKEN_TILE, D_MODEL), lambda i: (i, 0))
    half = pl.BlockSpec((TOKEN_TILE, GROUP_WIDTH), lambda i: (i, 0))
    return pl.pallas_call(
        _mix_ffn_kernel,
        grid=(n // TOKEN_TILE,),
        in_specs=[row, half, half, _resident((D_MIX, D_MODEL))] + _ffn_specs(),
        out_specs=row,
        out_shape=jax.ShapeDtypeStruct((n, D_MODEL), F32),
        scratch_shapes=[pltpu.VMEM((TOKEN_TILE, D_FF), BF16)],
        compiler_params=_params(1),
        name="mix_ffn",
    )(x2, oa, ob, wo, gain, wg, wu, wd)


def _head_sum_matrix():
    r = lax.broadcasted_iota(jnp.int32, (LANES, LANES), 0) // HEAD_DIM
    c = lax.broadcasted_iota(jnp.int32, (LANES, LANES), 1) // HEAD_DIM
    return jnp.where(r == c, 1.0, 0.0).astype(BF16)


def _head_rms(y, gain, hsum):
    hi, lo = _split2(y * y)
    ss = _dot(hi, hsum) + _dot(lo, hsum)
    return y * lax.rsqrt(ss * (1.0 / HEAD_DIM) + RMS_EPS) * gain


def _rope(y, cos, sin_signed, first_half):
    half = HEAD_DIM // 2
    partner = jnp.where(first_half, pltpu.roll(y, LANES - half, 1), pltpu.roll(y, half, 1))
    return y * cos + partner * sin_signed


def _proj_blocks(xn, w_ref, n_blocks):
    per = PROJ_CHUNK // LANES
    for c in range(0, n_blocks, per):
        width = min(per, n_blocks - c) * LANES
        y = _dot(xn, w_ref[:, c * LANES:c * LANES + width])
        for s in range(width // LANES):
            yield c + s, y[:, s * LANES:(s + 1) * LANES]


def _first_half_mask(rows):
    lane = lax.broadcasted_iota(jnp.int32, (rows, LANES), 1)
    return (lane % HEAD_DIM) < (HEAD_DIM // 2)


def _proj_ab_kernel(x_ref, g_ref, w_ref, gq_ref, gk_ref, cos_ref, sin_ref,
                    qkv_ref, qf_ref, km_ref):
    xn = _rms_rows(x_ref[...], g_ref[...]).astype(BF16)
    hsum = _head_sum_matrix()
    cos, sin = cos_ref[...], sin_ref[...]
    first_half = _first_half_mask(TOKEN_TILE)
    blocks_per_kind = D_MIX // LANES
    for cb, y in _proj_blocks(xn, w_ref, 3 * blocks_per_kind):
        kind, pp = divmod(cb, blocks_per_kind)
        if kind < 2 and pp >= N_PAIRS:
            gain = (gq_ref if kind == 0 else gk_ref)[...]
            y = _rope(_head_rms(y, gain, hsum), cos, sin, first_half)
            sl = slice((pp - N_PAIRS) * LANES, (pp - N_PAIRS + 1) * LANES)
            if kind == 0:
                qf_ref[:, sl] = y
            else:
                for jb in range(TOKEN_TILE // MOBA_BLOCK):
                    rows = y[jb * MOBA_BLOCK:(jb + 1) * MOBA_BLOCK, :]
                    km_ref[jb, :, sl] = jnp.mean(rows, axis=0, keepdims=True)
        if kind == 0:
            y = y * ATTN_SCALE
        qkv_ref[:, cb * LANES:(cb + 1) * LANES] = y.astype(BF16)


def _proj_cd_kernel(x_ref, g_ref, w_ref, gqc_ref, gkc_ref, gqd_ref, gkd_ref, bf_ref,
                    cos_ref, sin_ref, qkv_ref, lf_ref):
    xn = _rms_rows(x_ref[...], g_ref[...]).astype(BF16)
    hsum = _head_sum_matrix()
    cos, sin = cos_ref[...], sin_ref[...]
    first_half = _first_half_mask(TOKEN_TILE)
    blocks_per_kind = D_MIX // LANES
    for cb, y in _proj_blocks(xn, w_ref, 3 * blocks_per_kind + 1):
        kind, pp = divmod(cb, blocks_per_kind)
        if kind == 3:
            t = y + bf_ref[...]
            lf_ref[...] = jnp.minimum(t, 0.0) - jnp.log1p(jnp.exp(-jnp.abs(t)))
            continue
        if kind < 2:
            if pp < N_PAIRS:
                y = _head_rms(y, (gqc_ref if kind == 0 else gkc_ref)[...], hsum)
            else:
                y = _head_rms(y, (gqd_ref if kind == 0 else gkd_ref)[...], hsum)
                y = _rope(y, cos, sin, first_half)
        if kind == 0:
            y = y * ATTN_SCALE
        qkv_ref[:, cb * LANES:(cb + 1) * LANES] = y.astype(BF16)


def _proj_common_specs(seq, w_cols):
    row = pl.BlockSpec((TOKEN_TILE, D_MODEL), lambda i: (i, 0))
    tiles_per_seq = seq // TOKEN_TILE
    table = pl.BlockSpec((TOKEN_TILE, LANES), lambda i: (i % tiles_per_seq, 0))
    lane_vec = pl.BlockSpec((1, LANES), lambda i: (0, 0))
    gain = pl.BlockSpec((1, D_MODEL), lambda i: (0, 0))
    return row, gain, _resident((D_MODEL, w_cols)), lane_vec, table


def _proj_ab(x2, seq, gain, w, gq, gk, cos, sin):
    n = x2.shape[0]
    row, gspec, wspec, lane_vec, table = _proj_common_specs(seq, 3 * D_MIX)
    blocks = TOKEN_TILE // MOBA_BLOCK
    return pl.pallas_call(
        _proj_ab_kernel,
        grid=(n // TOKEN_TILE,),
        in_specs=[row, gspec, wspec, lane_vec, lane_vec, table, table],
        out_specs=[pl.BlockSpec((TOKEN_TILE, 3 * D_MIX), lambda i: (i, 0)),
                   pl.BlockSpec((TOKEN_TILE, GROUP_WIDTH), lambda i: (i, 0)),
                   pl.BlockSpec((blocks, 1, GROUP_WIDTH), lambda i: (i, 0, 0))],
        out_shape=[jax.ShapeDtypeStruct((n, 3 * D_MIX), BF16),
                   jax.ShapeDtypeStruct((n, GROUP_WIDTH), F32),
                   jax.ShapeDtypeStruct((n // MOBA_BLOCK, 1, GROUP_WIDTH), F32)],
        compiler_params=_params(1),
        name="proj_ab",
    )(x2, gain, w, gq, gk, cos, sin)


def _proj_cd(x2, seq, gain, w, gqc, gkc, gqd, gkd, bf, cos, sin):
    n = x2.shape[0]
    row, gspec, wspec, lane_vec, table = _proj_common_specs(seq, 3 * D_MIX + LANES)
    return pl.pallas_call(
        _proj_cd_kernel,
        grid=(n // TOKEN_TILE,),
        in_specs=[row, gspec, wspec, lane_vec, lane_vec, lane_vec, lane_vec, lane_vec, table, table],
        out_specs=[pl.BlockSpec((TOKEN_TILE, 3 * D_MIX), lambda i: (i, 0)),
                   pl.BlockSpec((TOKEN_TILE, LANES), lambda i: (i, 0))],
        out_shape=[jax.ShapeDtypeStruct((n, 3 * D_MIX), BF16),
                   jax.ShapeDtypeStruct((n, LANES), F32)],
        compiler_params=_params(1),
        name="proj_cd",
    )(x2, gain, w, gqc, gkc, gqd, gkd, bf, cos, sin)


def _cumsum_kernel(lf_ref, c_ref, carry_ref):
    @pl.when(pl.program_id(1) == 0)
    def _():
        carry_ref[...] = jnp.zeros_like(carry_ref)

    t = CUMSUM_TILE
    r = lax.broadcasted_iota(jnp.int32, (t, t), 0)
    c = lax.broadcasted_iota(jnp.int32, (t, t), 1)
    lower = jnp.where(c <= r, 1.0, 0.0).astype(BF16)
    lf = lf_ref[...]
    hi = lf.astype(BF16)
    mid, lo = _split2(lf - hi.astype(F32))
    out = _dot(lower, hi) + _dot(lower, mid) + _dot(lower, lo) + carry_ref[0:1, :]
    c_ref[...] = out
    carry_ref[0:1, :] = out[t - 1:t, :]


def _cumsum(lf3):
    b, s, _ = lf3.shape
    blk = pl.BlockSpec((None, CUMSUM_TILE, LANES), lambda bi, i: (bi, i, 0))
    return pl.pallas_call(
        _cumsum_kernel,
        grid=(b, s // CUMSUM_TILE),
        in_specs=[blk],
        out_specs=blk,
        out_shape=jax.ShapeDtypeStruct(lf3.shape, F32),
        scratch_shapes=[pltpu.VMEM((8, LANES), F32)],
        compiler_params=_params(2),
        name="forget_cumsum",
    )(lf3)


def _pair_queries(q, rows):
    head = lax.broadcasted_iota(jnp.int32, (rows, LANES), 1) // HEAD_DIM
    return head, [jnp.where(head == h, q, jnp.zeros_like(q)) for h in range(HEADS_PER_BLOCK)]


def _softmax_step(s, m, l, acc, vb):
    m_new = jnp.maximum(m, jnp.max(s, axis=1, keepdims=True))
    alpha = jnp.exp(m - m_new)
    p = jnp.exp(s - m_new)
    l = alpha * l + jnp.sum(p, axis=1, keepdims=True)
    acc = alpha * acc + _dot(p.astype(BF16), vb)
    return m_new, l, acc


def _softmax_init(rows):
    return (jnp.full((rows, 1), -jnp.inf, F32), jnp.zeros((rows, 1), F32),
            jnp.zeros((rows, LANES), F32))


def _loop_grouped(n, body, init, groups=(2, 1)):
    done, state = 0, init
    for g in groups:
        def grouped(c, st, g=g, done=done):
            for u in range(g):
                st = body(done + g * c + u, st)
            return st
        count = (n - done) // g
        state = lax.fori_loop(0, count, grouped, state)
        done = done + count * g
    return state


def _causal_bias(t):
    row = lax.broadcasted_iota(jnp.int32, (t, t), 0)
    col = lax.broadcasted_iota(jnp.int32, (t, t), 1)
    return jnp.where(col <= row, 0.0, -MASK_BIG)


def _store_heads(o_ref, head, outs):
    o_ref[...] = jnp.where(head == 0, outs[0], outs[1]).astype(o_ref.dtype)


def _sb_kernel(q_ref, sfx_ref, mask_ref, k_ref, v_ref, o_ref):
    t = SB_TILE
    i = pl.program_id(2)
    head, qhs = _pair_queries(q_ref[...], t)
    sfx = sfx_ref[...]
    lanes_of = lambda x: jnp.concatenate([x] * (t // LANES), axis=1)

    def block_sums(log_keep):
        hi, lo = _split2(log_keep)
        return _dot(jnp.concatenate([hi, lo], axis=1), sfx)

    def log_gates(qh, kb):
        z = _dot_nt(qh, kb)
        soft = jnp.log(1.0 + jnp.exp(-jnp.abs(z)))
        return jnp.minimum(z, 0.0) - soft, jnp.minimum(-z, 0.0) - soft

    which = jnp.minimum(i, 1)
    start = pl.multiple_of(jnp.maximum(i - 1, 0) * t, t)
    strict = mask_ref[which, 0]
    strict_neg = mask_ref[which, 1]
    kb = k_ref[pl.ds(start, 2 * t), :]
    vb = v_ref[pl.ds(start, 2 * t), :]
    carry = (i - 2,)
    for qh in qhs:
        log_beta, log_keep = log_gates(qh, kb)
        log_keep = log_keep * strict
        late = block_sums(log_keep[:, t:])
        early = block_sums(log_keep[:, :t])
        between = jnp.concatenate([early[:, :t] + lanes_of(late[:, t:]), late[:, :t]], axis=1)
        w = jnp.exp(log_beta + between + strict_neg)
        carry += (late[:, t:] + early[:, t:], _dot(w.astype(BF16), vb))

    def step(j, run, acc, qh):
        start = pl.multiple_of(j * t, t)
        log_beta, log_keep = log_gates(qh, k_ref[pl.ds(start, t), :])
        sums = block_sums(log_keep)
        w = jnp.exp(log_beta + lanes_of(run) + sums[:, :t])
        return run + sums[:, t:], acc + _dot(w.astype(BF16), v_ref[pl.ds(start, t), :])

    def cond(carry):
        return (carry[0] >= 0) & (jnp.maximum(jnp.max(carry[1]), jnp.max(carry[3])) > -SKIP_LOG)

    def body(carry):
        out = (carry[0] - 1,)
        for h, qh in enumerate(qhs):
            out += step(carry[0], carry[1 + 2 * h], carry[2 + 2 * h], qh)
        return out

    carry = lax.while_loop(cond, body, carry)
    _store_heads(o_ref, head, [carry[2], carry[4]])


def _score_bound(q, k_ref, knorm_ref, first_step, t):
    hsum = _head_sum_matrix()

    @pl.when(first_step)
    def _():
        def blk_max(j, best):
            kb = k_ref[pl.ds(pl.multiple_of(j * t, t), t), :].astype(F32)
            ss = _dot((kb * kb).astype(BF16), hsum)
            return jnp.maximum(best, jnp.max(ss, axis=0, keepdims=True))
        best = lax.fori_loop(0, k_ref.shape[0] // t, blk_max, jnp.zeros((1, LANES), F32))
        knorm_ref[...] = jnp.broadcast_to(best, knorm_ref.shape)

    qf = q.astype(F32)
    qn2 = _dot((qf * qf).astype(BF16), hsum)
    return 1.05 * jnp.sqrt(qn2 * knorm_ref[0:1, :]) + 1e-6


def _moba_kernel(q_ref, qf_ref, km_ref, onehot_ref, k_ref, v_ref, o_ref, knorm_ref):
    t, ch = MOBA_BLOCK, MOBA_CHUNK
    per = ch // t
    own = pl.program_id(2)
    nkb = km_ref.shape[0]
    q, qf = q_ref[...], qf_ref[...]
    head, qhs = _pair_queries(q, t)
    bound = _score_bound(q, k_ref, knorm_ref, own == 0, t)
    lane = lax.broadcasted_iota(jnp.int32, (t, LANES), 1)
    km_hi, km_lo = _split2(km_ref[...])
    blk_t = lax.broadcasted_iota(jnp.int32, (nkb, t), 0)
    blk_tf = blk_t.astype(F32)
    causal = _causal_bias(t)
    own_start = pl.multiple_of(own * t, t)
    k_own = k_ref[pl.ds(own_start, t), :]
    v_own = v_ref[pl.ds(own_start, t), :]
    one_own = onehot_ref[pl.ds(own_start, t), :]
    n_chunks = (own + per - 1) // per

    blocked = []
    for h in range(HEADS_PER_BLOCK):
        q_hi, q_lo = _split2(jnp.where(head == h, qf, 0.0))
        gate = _dot_nt(km_hi, q_hi) + _dot_nt(km_lo, q_hi) + _dot_nt(km_hi, q_lo)
        gate = jnp.where(blk_t < own, gate, -jnp.inf)
        sel = jnp.zeros((nkb, t), F32)
        for r in range(MOBA_TOPK):
            best = jnp.max(gate, axis=0, keepdims=True)
            first = jnp.min(jnp.where(gate == best, blk_tf, float(nkb)), axis=0, keepdims=True)
            hit = blk_tf == first
            sel = sel + jnp.where(hit, jnp.where(r < own, 1.0, 0.0), 0.0)
            gate = jnp.where(hit, -jnp.inf, gate)
        neg = jnp.concatenate([(sel - 1.0) * MASK_BIG, jnp.zeros((LANES - nkb, t), F32)], axis=0)
        blocked.append(neg.T)

    def rhs_of(start, rows):
        return jnp.concatenate([k_ref[pl.ds(start, rows), :], onehot_ref[pl.ds(start, rows), :]], axis=1)

    def shifted():
        lhs, lhs_own = [], []
        for h, qh in enumerate(qhs):
            shift = bound[:, h * HEAD_DIM:h * HEAD_DIM + 1]
            extra = jnp.where(lane == LANES - 1, -shift, blocked[h])
            lhs.append(jnp.concatenate([qh, extra.astype(BF16)], axis=1))
            extra_own = jnp.where(lane == LANES - 1, -shift, 0.0)
            lhs_own.append(jnp.concatenate([qh, extra_own.astype(BF16)], axis=1))

        def accumulate(s, l, acc, vb):
            p = jnp.exp(s)
            for c0 in range(0, p.shape[1], LANES):
                l = l + p[:, c0:c0 + LANES]
            return l, acc + _dot(p.astype(BF16), vb)

        zeros = jnp.zeros((t, LANES), F32)
        rhs_own = jnp.concatenate([k_own, one_own], axis=1)
        states = tuple(accumulate(_dot_nt(a, rhs_own) + causal, zeros, zeros, v_own) for a in lhs_own)

        def body(c, states):
            start = pl.multiple_of(c * ch, ch)
            rhs = rhs_of(start, ch)
            vb = v_ref[pl.ds(start, ch), :]
            return tuple(accumulate(_dot_nt(a, rhs), l, acc, vb) for a, (l, acc) in zip(lhs, states))

        states = _loop_grouped(n_chunks, body, states, groups=(4, 2, 1))
        outs = [acc / jnp.sum(l, axis=1, keepdims=True) for l, acc in states]
        return jnp.where(head == 0, outs[0], outs[1])

    def running_max():
        lhs = [jnp.concatenate([qh, neg.astype(BF16)], axis=1) for qh, neg in zip(qhs, blocked)]
        states = tuple(_softmax_step(_dot_nt(qh, k_own) + causal, *_softmax_init(t), v_own)
                       for qh in qhs)

        def body(c, states):
            start = pl.multiple_of(c * ch, ch)
            rhs = rhs_of(start, ch)
            vb = v_ref[pl.ds(start, ch), :]
            return tuple(_softmax_step(_dot_nt(a, rhs), *st, vb) for a, st in zip(lhs, states))

        states = lax.fori_loop(0, n_chunks, body, states)
        outs = [acc / l for _, l, acc in states]
        return jnp.where(head == 0, outs[0], outs[1])

    out = lax.cond(jnp.max(bound) <= SHIFT_LIMIT, shifted, running_max)
    o_ref[...] = out.astype(o_ref.dtype)


def _fox_kernel(q_ref, c_ref, ct_ref, causal_ref, k_ref, v_ref, o_ref, knorm_ref):
    t = ATTN_TILE
    pair = pl.program_id(1)
    i = pl.program_id(2)

    q = q_ref[...]
    head, qhs = _pair_queries(q, t)
    qk_bound = _score_bound(q, k_ref, knorm_ref, i == 0, t)
    c_blk = c_ref[...]
    lane = lax.broadcasted_iota(jnp.int32, (t, LANES), 1)
    c_qs = [jnp.sum(jnp.where(lane == HEADS_PER_BLOCK * pair + h, c_blk, 0.0), axis=1, keepdims=True)
            for h in range(HEADS_PER_BLOCK)]
    bounds = [qk_bound[:, h * HEAD_DIM:h * HEAD_DIM + 1] for h in range(HEADS_PER_BLOCK)]

    def block_min_c(h, j):
        start = pl.multiple_of(jnp.maximum(j, 0) * t, t)
        return jnp.min(ct_ref[h:h + 1, pl.ds(start, t)])

    def shifted():
        lhs = [jnp.concatenate([qh, jnp.where(lane == LANES - 1, -bounds[h], 0.0).astype(BF16)], axis=1)
               for h, qh in enumerate(qhs)]
        ones_col = {rows: jnp.where(lax.broadcasted_iota(jnp.int32, (rows, LANES), 1) == LANES - 1,
                                    1.0, 0.0).astype(BF16) for rows in (t, 2 * t)}
        level = [jnp.max(c_qs[h] + 2.0 * bounds[h]) for h in range(HEADS_PER_BLOCK)]

        def attend(start, rows, bias, states):
            rhs = jnp.concatenate([k_ref[pl.ds(start, rows), :], ones_col[rows]], axis=1)
            vb = v_ref[pl.ds(start, rows), :]
            out = ()
            for h, a in enumerate(lhs):
                s = _dot_nt(a, rhs) + (c_qs[h] - ct_ref[h:h + 1, pl.ds(start, rows)])
                p = jnp.exp(s if bias is None else s + bias)
                l = states[2 * h]
                for c0 in range(0, rows, LANES):
                    l = l + p[:, c0:c0 + LANES]
                out += (l, states[2 * h + 1] + _dot(p.astype(BF16), vb))
            return out

        zeros = jnp.zeros((t, LANES), F32)
        start = pl.multiple_of(jnp.maximum(i - 1, 0) * t, t)
        carry = (i - 2,) + attend(start, 2 * t, causal_ref[jnp.minimum(i, 1)], (zeros,) * 4)

        def cond(carry):
            j = carry[0]
            margin = [level[h] - block_min_c(h, j) for h in range(HEADS_PER_BLOCK)]
            return (j >= 0) & (jnp.maximum(margin[0], margin[1]) > -SKIP_LOG)

        def body(carry):
            start = pl.multiple_of(carry[0] * t, t)
            return (carry[0] - 1,) + attend(start, t, None, carry[1:])

        carry = lax.while_loop(cond, body, carry)
        outs = [carry[2 + 2 * h] / jnp.sum(carry[1 + 2 * h], axis=1, keepdims=True)
                for h in range(HEADS_PER_BLOCK)]
        return jnp.where(head == 0, outs[0], outs[1])

    def running_max():
        reach = [bounds[h] + c_qs[h] for h in range(HEADS_PER_BLOCK)]

        def step(j, states, extra):
            start = pl.multiple_of(j * t, t)
            kb = k_ref[pl.ds(start, t), :]
            vb = v_ref[pl.ds(start, t), :]
            out = ()
            for h, qh in enumerate(qhs):
                s = _dot_nt(qh, kb) + (c_qs[h] - ct_ref[h:h + 1, pl.ds(start, t)])
                if extra is not None:
                    s = s + extra
                out += _softmax_step(s, *states[3 * h:3 * h + 3], vb)
            return out

        carry = (i - 1,) + step(i, _softmax_init(t) * HEADS_PER_BLOCK, _causal_bias(t))

        def cond(carry):
            j = carry[0]
            margin = [jnp.max(reach[h] - carry[1 + 3 * h]) - block_min_c(h, j)
                      for h in range(HEADS_PER_BLOCK)]
            return (j >= 0) & (jnp.maximum(margin[0], margin[1]) > -SKIP_LOG)

        def body(carry):
            return (carry[0] - 1,) + step(carry[0], carry[1:], None)

        carry = lax.while_loop(cond, body, carry)
        outs = [carry[3 + 3 * h] / carry[2 + 3 * h] for h in range(HEADS_PER_BLOCK)]
        return jnp.where(head == 0, outs[0], outs[1])

    out = lax.cond(jnp.max(qk_bound) <= SHIFT_LIMIT, shifted, running_max)
    o_ref[...] = out.astype(o_ref.dtype)


def _dilated_kernel(q_ref, tbl_ref, k_ref, v_ref, o_ref, knorm_ref):
    t, ch = ATTN_TILE, DILATED_CHUNK
    i = pl.program_id(2)
    q = q_ref[...]
    head, qhs = _pair_queries(q, t)
    bound = _score_bound(q, k_ref, knorm_ref, i == 0, t)
    half = i // 2
    parity = i - 2 * half
    n_chunks = jnp.minimum(half + 1, tbl_ref.shape[0] // 2)

    def chunk(c):
        start = pl.multiple_of((half - c) * ch, ch)
        return k_ref[pl.ds(start, ch), :], v_ref[pl.ds(start, ch), :], tbl_ref[parity + 2 * c]

    def shifted():
        lane = lax.broadcasted_iota(jnp.int32, (t, LANES), 1)
        lhs = [jnp.concatenate(
            [qh, jnp.where(lane == LANES - 1, -bound[:, h * HEAD_DIM:h * HEAD_DIM + 1], 0.0).astype(BF16)],
            axis=1) for h, qh in enumerate(qhs)]
        ones_col = jnp.where(lax.broadcasted_iota(jnp.int32, (ch, LANES), 1) == LANES - 1, 1.0, 0.0).astype(BF16)

        def body(c, states):
            kb, vb, bias = chunk(c)
            rhs = jnp.concatenate([kb, ones_col], axis=1)
            out = ()
            for a, (l, acc) in zip(lhs, states):
                p = jnp.exp(_dot_nt(a, rhs) + bias)
                for c0 in range(0, ch, LANES):
                    l = l + p[:, c0:c0 + LANES]
                out += ((l, acc + _dot(p.astype(BF16), vb)),)
            return out

        zeros = jnp.zeros((t, LANES), F32)
        states = _loop_grouped(n_chunks, body, ((zeros, zeros),) * HEADS_PER_BLOCK)
        outs = [acc / jnp.sum(l, axis=1, keepdims=True) for l, acc in states]
        return jnp.where(head == 0, outs[0], outs[1])

    def running_max():
        def body(c, states):
            kb, vb, bias = chunk(c)
            return tuple(_softmax_step(_dot_nt(qh, kb) + bias, *st, vb) for qh, st in zip(qhs, states))

        states = lax.fori_loop(0, n_chunks, body, (_softmax_init(t),) * HEADS_PER_BLOCK)
        outs = [acc / l for _, l, acc in states]
        return jnp.where(head == 0, outs[0], outs[1])

    out = lax.cond(jnp.max(bound) <= SHIFT_LIMIT, shifted, running_max)
    o_ref[...] = out.astype(o_ref.dtype)


def _qkv_specs(seq, tq, group):
    blocks_per_kind = D_MIX // LANES
    off = group * N_PAIRS
    q = pl.BlockSpec((None, tq, LANES), lambda b, p, i: (b, i, off + p))
    k = pl.BlockSpec((None, seq, LANES), lambda b, p, i: (b, 0, blocks_per_kind + off + p))
    v = pl.BlockSpec((None, seq, LANES), lambda b, p, i: (b, 0, 2 * blocks_per_kind + off + p))
    return q, k, v


def _attn_call(kernel_fn, name, qkv3, tq, group, extra_specs, extra_args, scratch=()):
    b, seq, _ = qkv3.shape
    q, k, v = _qkv_specs(seq, tq, group)
    return pl.pallas_call(
        kernel_fn,
        grid=(b, N_PAIRS, seq // tq),
        in_specs=[q] + list(extra_specs) + [k, v],
        out_specs=pl.BlockSpec((None, tq, LANES), lambda b, p, i: (b, i, p)),
        out_shape=jax.ShapeDtypeStruct((b, seq, GROUP_WIDTH), BF16),
        scratch_shapes=list(scratch),
        compiler_params=_params(3),
        name=name,
    )(qkv3, *extra_args, qkv3, qkv3)


def _dilated_bias_table(t, ch):
    reach = max(w for w, _ in DILATED_BRANCHES)
    n = 2 * ((reach // t + 1 + ch // t) // 2)
    r = np.arange(t)[:, None]
    c = np.arange(ch)[None, :]
    out = np.zeros((n, t, ch), np.float32)
    for e in range(n):
        dist = e * t + r - c
        count = np.zeros((t, ch), np.float32)
        for window, dil in DILATED_BRANCHES:
            count += (dist >= 0) & (dist <= window) & (dist % dil == 0)
        out[e] = np.where(count > 0, np.log(np.maximum(count, 1.0)), -MASK_BIG)
    return out


def _suffix_matrix(t):
    j = np.arange(2 * t)[:, None] % t
    s = np.arange(t + LANES)[None, :]
    return jnp.asarray(((s >= t) | (j > s)).astype(np.float32), dtype=BF16)


def _first_step_masks(t, strict):
    r = np.arange(t)[:, None]
    c = np.arange(2 * t)[None, :]
    return np.stack([(c < r + d) if strict else (c <= r + d) for d in (0, t)]).astype(np.float32)


def _block_onehot(seq):
    assert seq // MOBA_BLOCK < LANES
    out = np.zeros((seq, LANES), np.float32)
    out[np.arange(seq), np.arange(seq) // MOBA_BLOCK] = 1.0
    out[:, LANES - 1] = 1.0
    return jnp.asarray(out, dtype=BF16)


def _rope_tables(seq):
    inv = 1.0 / (ROPE_THETA ** (jnp.arange(0, HEAD_DIM, 2, dtype=F32) / HEAD_DIM))
    ang = jnp.arange(seq, dtype=F32)[:, None] * inv[None, :]
    cos, sin = jnp.cos(ang), jnp.sin(ang)
    reps = LANES // HEAD_DIM
    cos_l = jnp.tile(jnp.concatenate([cos, cos], axis=1), (1, reps))
    sin_l = jnp.tile(jnp.concatenate([-sin, sin], axis=1), (1, reps))
    return cos_l, sin_l


def _lane_gain(g):
    return jnp.tile(g.astype(F32), LANES // HEAD_DIM)[None, :]


def kernel(x, norm_ffn1, ffn1_w_gate, ffn1_w_up, ffn1_w_down, norm_mix, w_in_ab, g_q_b, g_k_b,
           w_in_cd, b_f, g_q_c, g_k_c, g_q_d, g_k_d, w_out, norm_ffn2, ffn2_w_gate, ffn2_w_up,
           ffn2_w_down):
    b, seq, d = x.shape
    n = b * seq
    assert d == D_MODEL and n % TOKEN_TILE == 0 and seq % TOKEN_TILE == 0
    assert seq % MOBA_CHUNK == 0 and seq % SB_TILE == 0 and seq % DILATED_CHUNK == 0
    cos, sin = _rope_tables(seq)
    bf = lambda w: w.astype(BF16)
    row = lambda g: g.astype(F32)[None, :]
    x2 = x.reshape(n, d)

    x2 = _ffn(x2, row(norm_ffn1[0]), bf(ffn1_w_gate[0]), bf(ffn1_w_up[0]), bf(ffn1_w_down[0]))
    qkv, qf, km = _proj_ab(x2, seq, row(norm_mix[0]), bf(w_in_ab[0]),
                           _lane_gain(g_q_b[0]), _lane_gain(g_k_b[0]), cos, sin)
    qkv3 = qkv.reshape(b, seq, 3 * D_MIX)
    nkb = seq // MOBA_BLOCK
    key_norm = [pltpu.VMEM((8, LANES), F32)]
    sfx = _suffix_matrix(SB_TILE)
    strict = _first_step_masks(SB_TILE, True)
    sb_masks = jnp.asarray(np.stack([strict, (strict - 1.0) * MASK_BIG], axis=1))
    out_a = _attn_call(_sb_kernel, "stick_breaking", qkv3, SB_TILE, 0,
                       [_resident(sfx.shape), _resident(sb_masks.shape)], [sfx, sb_masks])
    out_b = _attn_call(
        _moba_kernel, "moba", qkv3, MOBA_BLOCK, 1,
        [pl.BlockSpec((None, MOBA_BLOCK, LANES), lambda b, p, i: (b, i, p)),
         pl.BlockSpec((None, nkb, LANES), lambda b, p, i: (b, 0, p)),
         _resident((seq, LANES))],
        [qf.reshape(b, seq, GROUP_WIDTH), km.reshape(b, nkb, GROUP_WIDTH), _block_onehot(seq)],
        scratch=key_norm)
    x2 = _mix_ffn(x2, out_a.reshape(n, GROUP_WIDTH), out_b.reshape(n, GROUP_WIDTH), bf(w_out[0]),
                  row(norm_ffn2[0]), bf(ffn2_w_gate[0]), bf(ffn2_w_up[0]), bf(ffn2_w_down[0]))

    x2 = _ffn(x2, row(norm_ffn1[1]), bf(ffn1_w_gate[1]), bf(ffn1_w_up[1]), bf(ffn1_w_down[1]))
    w_cd = jnp.pad(bf(w_in_cd[0]), ((0, 0), (0, LANES - GROUP_HEADS)))
    b_f_l = jnp.pad(b_f[0].astype(F32), (0, LANES - GROUP_HEADS))[None, :]
    qkv, lf = _proj_cd(x2, seq, row(norm_mix[1]), w_cd, _lane_gain(g_q_c[0]), _lane_gain(g_k_c[0]),
                       _lane_gain(g_q_d[0]), _lane_gain(g_k_d[0]), b_f_l, cos, sin)
    qkv3 = qkv.reshape(b, seq, 3 * D_MIX)
    c = _cumsum(lf.reshape(b, seq, LANES))
    c_t = jnp.swapaxes(c[:, :, :GROUP_HEADS], 1, 2).reshape(b, N_PAIRS, HEADS_PER_BLOCK, seq)
    causal = jnp.asarray((_first_step_masks(ATTN_TILE, False) - 1.0) * MASK_BIG)
    out_c = _attn_call(
        _fox_kernel, "forgetting", qkv3, ATTN_TILE, 0,
        [pl.BlockSpec((None, ATTN_TILE, LANES), lambda b, p, i: (b, i, 0)),
         pl.BlockSpec((None, None, HEADS_PER_BLOCK, seq), lambda b, p, i: (b, p, 0, 0)),
         _resident(causal.shape)],
        [c, c_t, causal], scratch=key_norm)
    table = jnp.asarray(_dilated_bias_table(ATTN_TILE, DILATED_CHUNK))
    out_d = _attn_call(_dilated_kernel, "dilated", qkv3, ATTN_TILE, 1, [_resident(table.shape)], [table],
                       scratch=key_norm)
    x2 = _mix_ffn(x2, out_c.reshape(n, GROUP_WIDTH), out_d.reshape(n, GROUP_WIDTH), bf(w_out[1]),
                  row(norm_ffn2[1]), bf(ffn2_w_gate[1]), bf(ffn2_w_up[1]), bf(ffn2_w_down[1]))
    return x2.reshape(b, seq, d)
```

```python
import functools

import numpy as np
import jax
import jax.numpy as jnp
from jax import lax
from jax.experimental import pallas as pl
from jax.experimental.pallas import tpu as pltpu

F32 = jnp.float32
BF16 = jnp.bfloat16

D_MODEL = 1024
HEAD_DIM = 64
D_MIX = 1024
D_FF = 2816
GROUP_HEADS = 8
GROUP_WIDTH = GROUP_HEADS * HEAD_DIM
LANES = 128
HEADS_PER_BLOCK = LANES // HEAD_DIM
N_PAIRS = GROUP_WIDTH // LANES
RMS_EPS = 1e-6
ROPE_THETA = 10000.0
FFN_RES_WEIGHT = 0.5
ATTN_SCALE = HEAD_DIM ** -0.5
MOBA_BLOCK = 256
MOBA_TOPK = 3
DILATED_BRANCHES = ((128, 1), (512, 4), (2048, 16))
SKIP_LOG = 88.0
VMEM_LIMIT_BYTES = 56 * 1024 * 1024

TOKEN_TILE = 512
FF_CHUNK = 256
PROJ_CHUNK = 256
SB_TILE = 256
ATTN_TILE = 256
MOBA_CHUNK = 2 * MOBA_BLOCK
DILATED_CHUNK = 2 * ATTN_TILE
CLASS_TILE = 128
DILATED_SLAB = 16 * CLASS_TILE
MASK_BIG = 1e30
SHIFT_LIMIT = 40.0
CUMSUM_TILE = 256

_NT = (((1,), (1,)), ((), ()))


def _params(n_grid):
    return pltpu.CompilerParams(
        dimension_semantics=("arbitrary",) * n_grid,
        vmem_limit_bytes=VMEM_LIMIT_BYTES)


def _resident(shape):
    nd = len(shape)
    return pl.BlockSpec(shape, lambda *_: (0,) * nd, pipeline_mode=pl.Buffered(1))


def _dot(a, b):
    return jnp.dot(a, b, preferred_element_type=F32)


def _dot_nt(a, b):
    return lax.dot_general(a, b, _NT, preferred_element_type=F32)


def _split2(x):
    hi = x.astype(BF16)
    lo = (x - hi.astype(F32)).astype(BF16)
    return hi, lo


def _rms_rows(x, gain):
    return x * lax.rsqrt(jnp.mean(x * x, axis=-1, keepdims=True) + RMS_EPS) * gain


def _swiglu_residual(x, g_ref, wg_ref, wu_ref, wd_ref, h_ref):
    xn = _rms_rows(x, g_ref[...]).astype(BF16)
    for c in range(D_FF // FF_CHUNK):
        sl = slice(c * FF_CHUNK, (c + 1) * FF_CHUNK)
        gate = _dot(xn, wg_ref[:, sl])
        up = _dot(xn, wu_ref[:, sl])
        h_ref[:, sl] = (gate * jax.nn.sigmoid(gate) * up).astype(BF16)
    return x + FFN_RES_WEIGHT * _dot(h_ref[...], wd_ref[...])


def _ffn_kernel(x_ref, g_ref, wg_ref, wu_ref, wd_ref, o_ref, h_ref):
    o_ref[...] = _swiglu_residual(x_ref[...], g_ref, wg_ref, wu_ref, wd_ref, h_ref)


def _mix_ffn_kernel(x_ref, a_ref, b_ref, wo_ref, g_ref, wg_ref, wu_ref, wd_ref, o_ref, h_ref):
    x = (x_ref[...] + _dot(a_ref[...], wo_ref[0:GROUP_WIDTH, :])
         + _dot(b_ref[...], wo_ref[GROUP_WIDTH:D_MIX, :]))
    o_ref[...] = _swiglu_residual(x, g_ref, wg_ref, wu_ref, wd_ref, h_ref)


def _ffn_specs():
    return [pl.BlockSpec((1, D_MODEL), lambda i: (0, 0)),
            _resident((D_MODEL, D_FF)), _resident((D_MODEL, D_FF)), _resident((D_FF, D_MODEL))]


def _ffn(x2, gain, wg, wu, wd):
    n = x2.shape[0]
    row = pl.BlockSpec((TOKEN_TILE, D_MODEL), lambda i: (i, 0))
    return pl.pallas_call(
        _ffn_kernel,
        grid=(n // TOKEN_TILE,),
        in_specs=[row] + _ffn_specs(),
        out_specs=row,
        out_shape=jax.ShapeDtypeStruct((n, D_MODEL), F32),
        scratch_shapes=[pltpu.VMEM((TOKEN_TILE, D_FF), BF16)],
        compiler_params=_params(1),
        name="ffn",
    )(x2, gain, wg, wu, wd)


def _mix_ffn(x2, oa, ob, wo, gain, wg, wu, wd):
    n = x2.shape[0]
    row = pl.BlockSpec((TOKEN_TILE, D_MODEL), lambda i: (i, 0))
    half = pl.BlockSpec((TOKEN_TILE, GROUP_WIDTH), lambda i: (i, 0))
    return pl.pallas_call(
        _mix_ffn_kernel,
        grid=(n // TOKEN_TILE,),
        in_specs=[row, half, half, _resident((D_MIX, D_MODEL))] + _ffn_specs(),
        out_specs=row,
        out_shape=jax.ShapeDtypeStruct((n, D_MODEL), F32),
        scratch_shapes=[pltpu.VMEM((TOKEN_TILE, D_FF), BF16)],
        compiler_params=_params(1),
        name="mix_ffn",
    )(x2, oa, ob, wo, gain, wg, wu, wd)


def _head_sum_matrix():
    r = lax.broadcasted_iota(jnp.int32, (LANES, LANES), 0) // HEAD_DIM
    c = lax.broadcasted_iota(jnp.int32, (LANES, LANES), 1) // HEAD_DIM
    return jnp.where(r == c, 1.0, 0.0).astype(BF16)


def _head_rms(y, gain, hsum):
    hi, lo = _split2(y * y)
    ss = _dot(hi, hsum) + _dot(lo, hsum)
    return y * lax.rsqrt(ss * (1.0 / HEAD_DIM) + RMS_EPS) * gain


def _rope(y, cos, sin_signed, first_half):
    half = HEAD_DIM // 2
    partner = jnp.where(first_half, pltpu.roll(y, LANES - half, 1), pltpu.roll(y, half, 1))
    return y * cos + partner * sin_signed


def _proj_blocks(xn, w_ref, n_blocks):
    per = PROJ_CHUNK // LANES
    for c in range(0, n_blocks, per):
        width = min(per, n_blocks - c) * LANES
        y = _dot(xn, w_ref[:, c * LANES:c * LANES + width])
        for s in range(width // LANES):
            yield c + s, y[:, s * LANES:(s + 1) * LANES]


def _first_half_mask(rows):
    lane = lax.broadcasted_iota(jnp.int32, (rows, LANES), 1)
    return (lane % HEAD_DIM) < (HEAD_DIM // 2)


def _proj_ab_kernel(x_ref, g_ref, w_ref, gq_ref, gk_ref, cos_ref, sin_ref,
                    qkv_ref, qf_ref, km_ref):
    xn = _rms_rows(x_ref[...], g_ref[...]).astype(BF16)
    hsum = _head_sum_matrix()
    cos, sin = cos_ref[...], sin_ref[...]
    first_half = _first_half_mask(TOKEN_TILE)
    blocks_per_kind = D_MIX // LANES
    for cb, y in _proj_blocks(xn, w_ref, 3 * blocks_per_kind):
        kind, pp = divmod(cb, blocks_per_kind)
        if kind < 2 and pp >= N_PAIRS:
            gain = (gq_ref if kind == 0 else gk_ref)[...]
            y = _rope(_head_rms(y, gain, hsum), cos, sin, first_half)
            sl = slice((pp - N_PAIRS) * LANES, (pp - N_PAIRS + 1) * LANES)
            if kind == 0:
                qf_ref[:, sl] = y
            else:
                for jb in range(TOKEN_TILE // MOBA_BLOCK):
                    rows = y[jb * MOBA_BLOCK:(jb + 1) * MOBA_BLOCK, :]
                    km_ref[jb, :, sl] = jnp.mean(rows, axis=0, keepdims=True)
        if kind == 0:
            y = y * ATTN_SCALE
        qkv_ref[:, cb * LANES:(cb + 1) * LANES] = y.astype(BF16)


def _proj_cd_kernel(x_ref, g_ref, w_ref, gqc_ref, gkc_ref, gqd_ref, gkd_ref, bf_ref,
                    cos_ref, sin_ref, qkv_ref, lf_ref, dil_ref):
    xn = _rms_rows(x_ref[...], g_ref[...]).astype(BF16)
    hsum = _head_sum_matrix()
    cos, sin = cos_ref[...], sin_ref[...]
    first_half = _first_half_mask(TOKEN_TILE)
    blocks_per_kind = D_MIX // LANES
    for cb, y in _proj_blocks(xn, w_ref, 3 * blocks_per_kind + 1):
        kind, pp = divmod(cb, blocks_per_kind)
        if kind == 3:
            t = y + bf_ref[...]
            lf_ref[...] = jnp.minimum(t, 0.0) - jnp.log1p(jnp.exp(-jnp.abs(t)))
            continue
        if kind < 2:
            if pp < N_PAIRS:
                y = _head_rms(y, (gqc_ref if kind == 0 else gkc_ref)[...], hsum)
            else:
                y = _head_rms(y, (gqd_ref if kind == 0 else gkd_ref)[...], hsum)
                y = _rope(y, cos, sin, first_half)
        if kind == 0:
            y = y * ATTN_SCALE
        qkv_ref[:, cb * LANES:(cb + 1) * LANES] = y.astype(BF16)
        if pp >= N_PAIRS:
            col = (kind * N_PAIRS + pp - N_PAIRS) * LANES
            dil_ref[:, col:col + LANES] = y


def _proj_common_specs(seq, w_cols):
    row = pl.BlockSpec((TOKEN_TILE, D_MODEL), lambda i: (i, 0))
    tiles_per_seq = seq // TOKEN_TILE
    table = pl.BlockSpec((TOKEN_TILE, LANES), lambda i: (i % tiles_per_seq, 0))
    lane_vec = pl.BlockSpec((1, LANES), lambda i: (0, 0))
    gain = pl.BlockSpec((1, D_MODEL), lambda i: (0, 0))
    return row, gain, _resident((D_MODEL, w_cols)), lane_vec, table


def _proj_ab(x2, seq, gain, w, gq, gk, cos, sin):
    n = x2.shape[0]
    row, gspec, wspec, lane_vec, table = _proj_common_specs(seq, 3 * D_MIX)
    blocks = TOKEN_TILE // MOBA_BLOCK
    return pl.pallas_call(
        _proj_ab_kernel,
        grid=(n // TOKEN_TILE,),
        in_specs=[row, gspec, wspec, lane_vec, lane_vec, table, table],
        out_specs=[pl.BlockSpec((TOKEN_TILE, 3 * D_MIX), lambda i: (i, 0)),
                   pl.BlockSpec((TOKEN_TILE, GROUP_WIDTH), lambda i: (i, 0)),
                   pl.BlockSpec((blocks, 1, GROUP_WIDTH), lambda i: (i, 0, 0))],
        out_shape=[jax.ShapeDtypeStruct((n, 3 * D_MIX), BF16),
                   jax.ShapeDtypeStruct((n, GROUP_WIDTH), F32),
                   jax.ShapeDtypeStruct((n // MOBA_BLOCK, 1, GROUP_WIDTH), F32)],
        compiler_params=_params(1),
        name="proj_ab",
    )(x2, gain, w, gq, gk, cos, sin)


def _proj_cd(x2, seq, gain, w, gqc, gkc, gqd, gkd, bf, cos, sin):
    n = x2.shape[0]
    row, gspec, wspec, lane_vec, table = _proj_common_specs(seq, 3 * D_MIX + LANES)
    return pl.pallas_call(
        _proj_cd_kernel,
        grid=(n // TOKEN_TILE,),
        in_specs=[row, gspec, wspec, lane_vec, lane_vec, lane_vec, lane_vec, lane_vec, table, table],
        out_specs=[pl.BlockSpec((TOKEN_TILE, 3 * D_MIX), lambda i: (i, 0)),
                   pl.BlockSpec((TOKEN_TILE, LANES), lambda i: (i, 0)),
                   pl.BlockSpec((TOKEN_TILE, 3 * GROUP_WIDTH), lambda i: (i, 0))],
        out_shape=[jax.ShapeDtypeStruct((n, 3 * D_MIX), BF16),
                   jax.ShapeDtypeStruct((n, LANES), F32),
                   jax.ShapeDtypeStruct((n, 3 * GROUP_WIDTH), F32)],
        compiler_params=_params(1),
        name="proj_cd",
    )(x2, gain, w, gqc, gkc, gqd, gkd, bf, cos, sin)


def _cumsum_kernel(lf_ref, c_ref, carry_ref):
    @pl.when(pl.program_id(1) == 0)
    def _():
        carry_ref[...] = jnp.zeros_like(carry_ref)

    t = CUMSUM_TILE
    r = lax.broadcasted_iota(jnp.int32, (t, t), 0)
    c = lax.broadcasted_iota(jnp.int32, (t, t), 1)
    lower = jnp.where(c <= r, 1.0, 0.0).astype(BF16)
    lf = lf_ref[...]
    hi = lf.astype(BF16)
    mid, lo = _split2(lf - hi.astype(F32))
    out = _dot(lower, hi) + _dot(lower, mid) + _dot(lower, lo) + carry_ref[0:1, :]
    c_ref[...] = out
    carry_ref[0:1, :] = out[t - 1:t, :]


def _cumsum(lf3):
    b, s, _ = lf3.shape
    blk = pl.BlockSpec((None, CUMSUM_TILE, LANES), lambda bi, i: (bi, i, 0))
    return pl.pallas_call(
        _cumsum_kernel,
        grid=(b, s // CUMSUM_TILE),
        in_specs=[blk],
        out_specs=blk,
        out_shape=jax.ShapeDtypeStruct(lf3.shape, F32),
        scratch_shapes=[pltpu.VMEM((8, LANES), F32)],
        compiler_params=_params(2),
        name="forget_cumsum",
    )(lf3)


def _pair_queries(q, rows):
    head = lax.broadcasted_iota(jnp.int32, (rows, LANES), 1) // HEAD_DIM
    return head, [jnp.where(head == h, q, jnp.zeros_like(q)) for h in range(HEADS_PER_BLOCK)]


def _softmax_step(s, m, l, acc, vb):
    m_new = jnp.maximum(m, jnp.max(s, axis=1, keepdims=True))
    alpha = jnp.exp(m - m_new)
    p = jnp.exp(s - m_new)
    l = alpha * l + jnp.sum(p, axis=1, keepdims=True)
    acc = alpha * acc + _dot(p.astype(BF16), vb)
    return m_new, l, acc


def _softmax_init(rows):
    return (jnp.full((rows, 1), -jnp.inf, F32), jnp.zeros((rows, 1), F32),
            jnp.zeros((rows, LANES), F32))


def _loop_grouped(n, body, init, groups=(2, 1)):
    done, state = 0, init
    for g in groups:
        def grouped(c, st, g=g, done=done):
            for u in range(g):
                st = body(done + g * c + u, st)
            return st
        count = (n - done) // g
        state = lax.fori_loop(0, count, grouped, state)
        done = done + count * g
    return state


def _causal_bias(t):
    row = lax.broadcasted_iota(jnp.int32, (t, t), 0)
    col = lax.broadcasted_iota(jnp.int32, (t, t), 1)
    return jnp.where(col <= row, 0.0, -MASK_BIG)


def _values_with_ones(vb):
    head = lax.broadcasted_iota(jnp.int32, vb.shape, 1) // HEAD_DIM
    return [jnp.where(head == h, vb, jnp.ones_like(vb)) for h in range(HEADS_PER_BLOCK)]


def _normalize_heads(head, accs):
    outs = [acc / pltpu.roll(acc, HEAD_DIM, 1) for acc in accs]
    return jnp.where(head == 0, outs[0], outs[1])


def _store_heads(o_ref, head, outs):
    o_ref[...] = jnp.where(head == 0, outs[0], outs[1]).astype(o_ref.dtype)


def _sb_kernel(q_ref, sfx_ref, mask_ref, k_ref, v_ref, o_ref):
    t = SB_TILE
    i = pl.program_id(2)
    head, qhs = _pair_queries(q_ref[...], t)
    sfx = sfx_ref[...]
    lanes_of = lambda x: jnp.concatenate([x] * (t // LANES), axis=1)

    def block_sums(log_keep):
        hi, lo = _split2(log_keep)
        return _dot(jnp.concatenate([hi, lo], axis=1), sfx)

    def log_gates(qh, kb):
        z = _dot_nt(qh, kb)
        soft = jnp.log(1.0 + jnp.exp(-jnp.abs(z)))
        return jnp.minimum(z, 0.0) - soft, jnp.minimum(-z, 0.0) - soft

    which = jnp.minimum(i, 1)
    start = pl.multiple_of(jnp.maximum(i - 1, 0) * t, t)
    strict = mask_ref[which, 0]
    strict_neg = mask_ref[which, 1]
    kb = k_ref[pl.ds(start, 2 * t), :]
    vb = v_ref[pl.ds(start, 2 * t), :]
    carry = (i - 2,)
    for qh in qhs:
        log_beta, log_keep = log_gates(qh, kb)
        log_keep = log_keep * strict
        late = block_sums(log_keep[:, t:])
        early = block_sums(log_keep[:, :t])
        between = jnp.concatenate([early[:, :t] + lanes_of(late[:, t:]), late[:, :t]], axis=1)
        w = jnp.exp(log_beta + between + strict_neg)
        carry += (late[:, t:] + early[:, t:], _dot(w.astype(BF16), vb))

    def step(j, run, acc, qh):
        start = pl.multiple_of(j * t, t)
        log_beta, log_keep = log_gates(qh, k_ref[pl.ds(start, t), :])
        sums = block_sums(log_keep)
        w = jnp.exp(log_beta + lanes_of(run) + sums[:, :t])
        return run + sums[:, t:], acc + _dot(w.astype(BF16), v_ref[pl.ds(start, t), :])

    def cond(carry):
        return (carry[0] >= 0) & (jnp.maximum(jnp.max(carry[1]), jnp.max(carry[3])) > -SKIP_LOG)

    def body(carry):
        out = (carry[0] - 1,)
        for h, qh in enumerate(qhs):
            out += step(carry[0], carry[1 + 2 * h], carry[2 + 2 * h], qh)
        return out

    carry = lax.while_loop(cond, body, carry)
    _store_heads(o_ref, head, [carry[2], carry[4]])


def _score_bound_row(g_q, g_k):
    bound = 1.05 * HEAD_DIM * ATTN_SCALE * jnp.max(jnp.abs(g_q)) * jnp.max(jnp.abs(g_k))
    return jnp.full((1, LANES), bound, F32)


def _moba_kernel(bound_ref, q_ref, qf_ref, km_ref, onehot_ref, k_ref, v_ref, o_ref):
    t, ch = MOBA_BLOCK, MOBA_CHUNK
    per = ch // t
    own = pl.program_id(2)
    nkb = km_ref.shape[0]
    q, qf = q_ref[...], qf_ref[...]
    head, qhs = _pair_queries(q, t)
    bound = bound_ref[...][:, 0:1]
    lane = lax.broadcasted_iota(jnp.int32, (t, LANES), 1)
    km_hi, km_lo = _split2(km_ref[...])
    blk_t = lax.broadcasted_iota(jnp.int32, (nkb, t), 0)
    blk_tf = blk_t.astype(F32)
    causal = _causal_bias(t)
    own_start = pl.multiple_of(own * t, t)
    k_own = k_ref[pl.ds(own_start, t), :]
    v_own = v_ref[pl.ds(own_start, t), :]
    one_own = onehot_ref[pl.ds(own_start, t), :]
    n_chunks = (own + per - 1) // per

    blocked = []
    for h in range(HEADS_PER_BLOCK):
        q_hi, q_lo = _split2(jnp.where(head == h, qf, 0.0))
        gate = _dot_nt(km_hi, q_hi) + _dot_nt(km_lo, q_hi) + _dot_nt(km_hi, q_lo)
        gate = jnp.where(blk_t < own, gate, -jnp.inf)
        sel = jnp.zeros((nkb, t), F32)
        for r in range(MOBA_TOPK):
            best = jnp.max(gate, axis=0, keepdims=True)
            first = jnp.min(jnp.where(gate == best, blk_tf, float(nkb)), axis=0, keepdims=True)
            hit = blk_tf == first
            sel = sel + jnp.where(hit, jnp.where(r < own, 1.0, 0.0), 0.0)
            gate = jnp.where(hit, -jnp.inf, gate)
        neg = jnp.concatenate([(sel - 1.0) * MASK_BIG, jnp.zeros((LANES - nkb, t), F32)], axis=0)
        blocked.append(neg.T)

    def rhs_of(start, rows):
        return jnp.concatenate([k_ref[pl.ds(start, rows), :], onehot_ref[pl.ds(start, rows), :]], axis=1)

    def shifted():
        lhs, lhs_own = [], []
        for h, qh in enumerate(qhs):
            extra = jnp.where(lane == LANES - 1, -bound, blocked[h])
            lhs.append(jnp.concatenate([qh, extra.astype(BF16)], axis=1))
            extra_own = jnp.where(lane == LANES - 1, -bound, 0.0)
            lhs_own.append(jnp.concatenate([qh, extra_own.astype(BF16)], axis=1))

        rhs_own = jnp.concatenate([k_own, one_own], axis=1)
        v_owns = _values_with_ones(v_own)
        accs = tuple(_dot(jnp.exp(_dot_nt(a, rhs_own) + causal).astype(BF16), vh)
                     for a, vh in zip(lhs_own, v_owns))

        def body(c, accs):
            start = pl.multiple_of(c * ch, ch)
            rhs = rhs_of(start, ch)
            vhs = _values_with_ones(v_ref[pl.ds(start, ch), :])
            return tuple(acc + _dot(jnp.exp(_dot_nt(a, rhs)).astype(BF16), vh)
                         for a, vh, acc in zip(lhs, vhs, accs))

        accs = _loop_grouped(n_chunks, body, accs, groups=(4, 2, 1))
        return _normalize_heads(head, accs)

    def running_max():
        lhs = [jnp.concatenate([qh, neg.astype(BF16)], axis=1) for qh, neg in zip(qhs, blocked)]
        states = tuple(_softmax_step(_dot_nt(qh, k_own) + causal, *_softmax_init(t), v_own)
                       for qh in qhs)

        def body(c, states):
            start = pl.multiple_of(c * ch, ch)
            rhs = rhs_of(start, ch)
            vb = v_ref[pl.ds(start, ch), :]
            return tuple(_softmax_step(_dot_nt(a, rhs), *st, vb) for a, st in zip(lhs, states))

        states = lax.fori_loop(0, n_chunks, body, states)
        outs = [acc / l for _, l, acc in states]
        return jnp.where(head == 0, outs[0], outs[1])

    out = lax.cond(jnp.max(bound) <= SHIFT_LIMIT, shifted, running_max)
    o_ref[...] = out.astype(o_ref.dtype)


def _fox_kernel(bound_ref, q_ref, c_ref, ct_ref, causal_ref, k_ref, v_ref, o_ref):
    t = ATTN_TILE
    pair = pl.program_id(1)
    i = pl.program_id(2)

    q = q_ref[...]
    head, qhs = _pair_queries(q, t)
    qk_bound = bound_ref[...][:, 0:1]
    c_blk = c_ref[...]
    lane = lax.broadcasted_iota(jnp.int32, (t, LANES), 1)
    c_qs = [jnp.sum(jnp.where(lane == HEADS_PER_BLOCK * pair + h, c_blk, 0.0), axis=1, keepdims=True)
            for h in range(HEADS_PER_BLOCK)]
    bounds = [qk_bound] * HEADS_PER_BLOCK

    def block_min_c(h, j):
        start = pl.multiple_of(jnp.maximum(j, 0) * t, t)
        return jnp.min(ct_ref[h:h + 1, pl.ds(start, t)])

    def shifted():
        lhs = [jnp.concatenate([qh, jnp.where(lane == LANES - 1, -bounds[h], 0.0).astype(BF16)], axis=1)
               for h, qh in enumerate(qhs)]
        ones_col = {rows: jnp.where(lax.broadcasted_iota(jnp.int32, (rows, LANES), 1) == LANES - 1,
                                    1.0, 0.0).astype(BF16) for rows in (t, 2 * t)}
        level = [jnp.max(c_qs[h] + 2.0 * bounds[h]) for h in range(HEADS_PER_BLOCK)]

        def attend(start, rows, bias, accs):
            rhs = jnp.concatenate([k_ref[pl.ds(start, rows), :], ones_col[rows]], axis=1)
            vhs = _values_with_ones(v_ref[pl.ds(start, rows), :])
            out = ()
            for h, a in enumerate(lhs):
                s = _dot_nt(a, rhs) + (c_qs[h] - ct_ref[h:h + 1, pl.ds(start, rows)])
                p = jnp.exp(s if bias is None else s + bias)
                out += (accs[h] + _dot(p.astype(BF16), vhs[h]),)
            return out

        zeros = jnp.zeros((t, LANES), F32)
        start = pl.multiple_of(jnp.maximum(i - 1, 0) * t, t)
        carry = (i - 2,) + attend(start, 2 * t, causal_ref[jnp.minimum(i, 1)], (zeros,) * 2)

        def cond(carry):
            j = carry[0]
            margin = [level[h] - block_min_c(h, j) for h in range(HEADS_PER_BLOCK)]
            return (j >= 0) & (jnp.maximum(margin[0], margin[1]) > -SKIP_LOG)

        def body(carry):
            start = pl.multiple_of(carry[0] * t, t)
            return (carry[0] - 1,) + attend(start, t, None, carry[1:])

        carry = lax.while_loop(cond, body, carry)
        return _normalize_heads(head, carry[1:])

    def running_max():
        reach =[bounds[h] + c_qs[h] for h in range(HEADS_PER_BLOCK)]

        def step(j, states, extra):
            start = pl.multiple_of(j * t, t)
            kb = k_ref[pl.ds(start, t), :]
            vb = v_ref[pl.ds(start, t), :]
            out = ()
            for h, qh in enumerate(qhs):
                s = _dot_nt(qh, kb) + (c_qs[h] - ct_ref[h:h + 1, pl.ds(start, t)])
                if extra is not None:
                    s = s + extra
                out += _softmax_step(s, *states[3 * h:3 * h + 3], vb)
            return out

        carry = (i - 1,) + step(i, _softmax_init(t) * HEADS_PER_BLOCK, _causal_bias(t))

        def cond(carry):
            j = carry[0]
            margin = [jnp.max(reach[h] - carry[1 + 3 * h]) - block_min_c(h, j)
                      for h in range(HEADS_PER_BLOCK)]
            return (j >= 0) & (jnp.maximum(margin[0], margin[1]) > -SKIP_LOG)

        def body(carry):
            return (carry[0] - 1,) + step(carry[0], carry[1:], None)

        carry = lax.while_loop(cond, body, carry)
        outs = [carry[3 + 3 * h] / carry[2 + 3 * h] for h in range(HEADS_PER_BLOCK)]
        return jnp.where(head == 0, outs[0], outs[1])

    out = lax.cond(jnp.max(qk_bound) <= SHIFT_LIMIT, shifted, running_max)
    o_ref[...] = out.astype(o_ref.dtype)


def _dilated_kernel(bound_ref, q_ref, tbl_ref, k_ref, v_ref, o_ref):
    t, ch = ATTN_TILE, DILATED_CHUNK
    i = pl.program_id(2)
    q = q_ref[...]
    head, qhs = _pair_queries(q, t)
    bound = bound_ref[...][:, 0:1]
    half = i // 2
    parity = i - 2 * half
    n_chunks = jnp.minimum(half + 1, tbl_ref.shape[0] // 2)

    def chunk(c):
        start = pl.multiple_of((half - c) * ch, ch)
        return k_ref[pl.ds(start, ch), :], v_ref[pl.ds(start, ch), :], tbl_ref[parity + 2 * c]

    def shifted():
        lane = lax.broadcasted_iota(jnp.int32, (t, LANES), 1)
        lhs = [jnp.concatenate(
            [qh, jnp.where(lane == LANES - 1, -bound, 0.0).astype(BF16)],
            axis=1) for h, qh in enumerate(qhs)]
        ones_col = jnp.where(lax.broadcasted_iota(jnp.int32, (ch, LANES), 1) == LANES - 1, 1.0, 0.0).astype(BF16)

        def body(c, states):
            kb, vb, bias = chunk(c)
            rhs = jnp.concatenate([kb, ones_col], axis=1)
            out = ()
            for a, (l, acc) in zip(lhs, states):
                p = jnp.exp(_dot_nt(a, rhs) + bias)
                for c0 in range(0, ch, LANES):
                    l = l + p[:, c0:c0 + LANES]
                out += ((l, acc + _dot(p.astype(BF16), vb)),)
            return out

        zeros = jnp.zeros((t, LANES), F32)
        states = _loop_grouped(n_chunks, body, ((zeros, zeros),) * HEADS_PER_BLOCK)
        outs = [acc / jnp.sum(l, axis=1, keepdims=True) for l, acc in states]
        return jnp.where(head == 0, outs[0], outs[1])

    def running_max():
        def body(c, states):
            kb, vb, bias = chunk(c)
            return tuple(_softmax_step(_dot_nt(qh, kb) + bias, *st, vb) for qh, st in zip(qhs, states))

        states = lax.fori_loop(0, n_chunks, body, (_softmax_init(t),) * HEADS_PER_BLOCK)
        outs = [acc / l for _, l, acc in states]
        return jnp.where(head == 0, outs[0], outs[1])

    out = lax.cond(jnp.max(bound) <= SHIFT_LIMIT, shifted, running_max)
    o_ref[...] = out.astype(o_ref.dtype)


def _dilated_class_kernel(shift_ref, band_ref, q_ref, k_ref, v_ref, o_ref, tot_ref):
    ct = CLASS_TILE
    slab = pl.program_id(2)
    head = lax.broadcasted_iota(jnp.int32, (ct, LANES), 1) // HEAD_DIM
    shift = shift_ref[...][:, 0:1]
    band_first = band_ref[0] - shift
    band = band_ref[1] - shift
    band_edge = jnp.where(slab == 0, band_first, band)

    def rows(ref, start, count, d):
        return ref[pl.ds(start, count), :] if d == 1 else ref[pl.ds(start, count, stride=d), :]

    for branch, (window, d) in enumerate(DILATED_BRANCHES):
        assert window == ct * d and DILATED_SLAB % (ct * d) == 0
        for r in range(d):
            for u in range(DILATED_SLAB // (ct * d)):
                row0 = r + d * ct * u
                qb = rows(q_ref, row0, ct, d).astype(BF16)
                key0 = slab * DILATED_SLAB + (row0 - d * ct)
                start = jnp.maximum(key0, r) if u == 0 else key0
                kb = rows(k_ref, start, 2 * ct, d).astype(BF16)
                vhs = _values_with_ones(rows(v_ref, start, 2 * ct, d).astype(BF16))
                for h in range(HEADS_PER_BLOCK):
                    qh = jnp.where(head == h, qb, jnp.zeros_like(qb))
                    p = jnp.exp(_dot_nt(qh, kb) + (band_edge if u == 0 else band))
                    acc = _dot(p.astype(BF16), vhs[h])
                    if branch == 0:
                        tot_ref[h, pl.ds(row0, ct), :] = acc
                    elif d == 1:
                        tot_ref[h, pl.ds(row0, ct), :] += acc
                    else:
                        tot_ref[h, pl.ds(row0, ct, stride=d), :] += acc

    slab_head = lax.broadcasted_iota(jnp.int32, (DILATED_SLAB, LANES), 1) // HEAD_DIM
    out = _normalize_heads(slab_head, [tot_ref[h] for h in range(HEADS_PER_BLOCK)])
    o_ref[...] = out.astype(o_ref.dtype)


def _dilated_classes(dil3, shift_row):
    b, seq, _ = dil3.shape
    ct = CLASS_TILE
    r = np.arange(ct)[:, None]
    c = np.arange(2 * ct)[None, :]
    band = np.stack([c <= r, (c >= r) & (c <= r + ct)]).astype(np.float32)
    band = jnp.asarray((band - 1.0) * MASK_BIG)
    return pl.pallas_call(
        _dilated_class_kernel,
        grid=(b, N_PAIRS, seq // DILATED_SLAB),
        in_specs=[pl.BlockSpec((1, LANES), lambda b, p, i: (0, 0)),
                  _resident(band.shape),
                  pl.BlockSpec((None, DILATED_SLAB, LANES), lambda b, p, i: (b, i, p)),
                  pl.BlockSpec((None, seq, LANES), lambda b, p, i: (b, 0, N_PAIRS + p)),
                  pl.BlockSpec((None, seq, LANES), lambda b, p, i: (b, 0, 2 * N_PAIRS + p))],
        out_specs=pl.BlockSpec((None, DILATED_SLAB, LANES), lambda b, p, i: (b, i, p)),
        out_shape=jax.ShapeDtypeStruct((b, seq, GROUP_WIDTH), BF16),
        scratch_shapes=[pltpu.VMEM((HEADS_PER_BLOCK, DILATED_SLAB, LANES), F32)],
        compiler_params=_params(3),
        name="dilated_classes",
    )(shift_row, band, dil3, dil3, dil3)


def _qkv_specs(seq, tq, group):
    blocks_per_kind = D_MIX // LANES
    off = group * N_PAIRS
    q = pl.BlockSpec((None, tq, LANES), lambda b, p, i: (b, i, off + p))
    k = pl.BlockSpec((None, seq, LANES), lambda b, p, i: (b, 0, blocks_per_kind + off + p))
    v = pl.BlockSpec((None, seq, LANES), lambda b, p, i: (b, 0, 2 * blocks_per_kind + off + p))
    return q, k, v


def _attn_call(kernel_fn, name, qkv3, tq, group, extra_specs, extra_args, bound_row=None):
    b, seq, _ = qkv3.shape
    q, k, v = _qkv_specs(seq, tq, group)
    lead_specs = [] if bound_row is None else [pl.BlockSpec((1, LANES), lambda b, p, i: (0, 0))]
    lead_args = [] if bound_row is None else [bound_row]
    return pl.pallas_call(
        kernel_fn,
        grid=(b, N_PAIRS, seq // tq),
        in_specs=lead_specs + [q] + list(extra_specs) + [k, v],
        out_specs=pl.BlockSpec((None, tq, LANES), lambda b, p, i: (b, i, p)),
        out_shape=jax.ShapeDtypeStruct((b, seq, GROUP_WIDTH), BF16),
        compiler_params=_params(3),
        name=name,
    )(*lead_args, qkv3, *extra_args, qkv3, qkv3)


def _dilated_bias_table(t, ch):
    reach = max(w for w, _ in DILATED_BRANCHES)
    n = 2 * ((reach // t + 1 + ch // t) // 2)
    r = np.arange(t)[:, None]
    c = np.arange(ch)[None, :]
    out = np.zeros((n, t, ch), np.float32)
    for e in range(n):
        dist = e * t + r - c
        count = np.zeros((t, ch), np.float32)
        for window, dil in DILATED_BRANCHES:
            count += (dist >= 0) & (dist <= window) & (dist % dil == 0)
        out[e] = np.where(count > 0, np.log(np.maximum(count, 1.0)), -MASK_BIG)
    return out


def _suffix_matrix(t):
    j = np.arange(2 * t)[:, None] % t
    s = np.arange(t + LANES)[None, :]
    return jnp.asarray(((s >= t) | (j > s)).astype(np.float32), dtype=BF16)


def _first_step_masks(t, strict):
    r = np.arange(t)[:, None]
    c = np.arange(2 * t)[None, :]
    return np.stack([(c < r + d) if strict else (c <= r + d) for d in (0, t)]).astype(np.float32)


def _block_onehot(seq):
    assert seq // MOBA_BLOCK < LANES
    out = np.zeros((seq, LANES), np.float32)
    out[np.arange(seq), np.arange(seq) // MOBA_BLOCK] = 1.0
    out[:, LANES - 1] = 1.0
    return jnp.asarray(out, dtype=BF16)


def _rope_tables(seq):
    inv = 1.0 / (ROPE_THETA ** (jnp.arange(0, HEAD_DIM, 2, dtype=F32) / HEAD_DIM))
    ang = jnp.arange(seq, dtype=F32)[:, None] * inv[None, :]
    cos, sin = jnp.cos(ang), jnp.sin(ang)
    reps = LANES // HEAD_DIM
    cos_l = jnp.tile(jnp.concatenate([cos, cos], axis=1), (1, reps))
    sin_l = jnp.tile(jnp.concatenate([-sin, sin], axis=1), (1, reps))
    return cos_l, sin_l


def _lane_gain(g):
    return jnp.tile(g.astype(F32), LANES // HEAD_DIM)[None, :]


def kernel(x, norm_ffn1, ffn1_w_gate, ffn1_w_up, ffn1_w_down, norm_mix, w_in_ab, g_q_b, g_k_b,
           w_in_cd, b_f, g_q_c, g_k_c, g_q_d, g_k_d, w_out, norm_ffn2, ffn2_w_gate, ffn2_w_up,
           ffn2_w_down):
    b, seq, d = x.shape
    n = b * seq
    assert d == D_MODEL and n % TOKEN_TILE == 0 and seq % TOKEN_TILE == 0
    assert seq % MOBA_CHUNK == 0 and seq % SB_TILE == 0 and seq % DILATED_CHUNK == 0
    cos, sin = _rope_tables(seq)
    bf = lambda w: w.astype(BF16)
    row = lambda g: g.astype(F32)[None, :]
    x2 = x.reshape(n, d)

    x2 = _ffn(x2, row(norm_ffn1[0]), bf(ffn1_w_gate[0]), bf(ffn1_w_up[0]), bf(ffn1_w_down[0]))
    qkv, qf, km = _proj_ab(x2, seq, row(norm_mix[0]), bf(w_in_ab[0]),
                           _lane_gain(g_q_b[0]), _lane_gain(g_k_b[0]), cos, sin)
    qkv3 = qkv.reshape(b, seq, 3 * D_MIX)
    nkb = seq // MOBA_BLOCK
    sfx =_suffix_matrix(SB_TILE)
    strict = _first_step_masks(SB_TILE, True)
    sb_masks = jnp.asarray(np.stack([strict, (strict - 1.0) * MASK_BIG], axis=1))
    out_a = _attn_call(_sb_kernel, "stick_breaking", qkv3, SB_TILE, 0,
                       [_resident(sfx.shape), _resident(sb_masks.shape)], [sfx, sb_masks])
    out_b = _attn_call(
        _moba_kernel, "moba", qkv3, MOBA_BLOCK, 1,
        [pl.BlockSpec((None, MOBA_BLOCK, LANES), lambda b, p, i: (b, i, p)),
         pl.BlockSpec((None, nkb, LANES), lambda b, p, i: (b, 0, p)),
         _resident((seq, LANES))],
        [qf.reshape(b, seq, GROUP_WIDTH), km.reshape(b, nkb, GROUP_WIDTH), _block_onehot(seq)],
        bound_row=_score_bound_row(g_q_b[0], g_k_b[0]))
    x2 = _mix_ffn(x2, out_a.reshape(n, GROUP_WIDTH), out_b.reshape(n, GROUP_WIDTH), bf(w_out[0]),
                  row(norm_ffn2[0]), bf(ffn2_w_gate[0]), bf(ffn2_w_up[0]), bf(ffn2_w_down[0]))

    x2 = _ffn(x2, row(norm_ffn1[1]), bf(ffn1_w_gate[1]), bf(ffn1_w_up[1]), bf(ffn1_w_down[1]))
    w_cd = jnp.pad(bf(w_in_cd[0]), ((0, 0), (0, LANES - GROUP_HEADS)))
    b_f_l = jnp.pad(b_f[0].astype(F32), (0, LANES - GROUP_HEADS))[None, :]
    qkv, lf, dil = _proj_cd(
        x2, seq, row(norm_mix[1]), w_cd, _lane_gain(g_q_c[0]), _lane_gain(g_k_c[0]),
        _lane_gain(g_q_d[0]), _lane_gain(g_k_d[0]), b_f_l, cos, sin)
    qkv3 = qkv.reshape(b, seq, 3 * D_MIX)
    c = _cumsum(lf.reshape(b, seq, LANES))
    c_t = jnp.swapaxes(c[:, :, :GROUP_HEADS], 1, 2).reshape(b, N_PAIRS, HEADS_PER_BLOCK, seq)
    causal = jnp.asarray((_first_step_masks(ATTN_TILE, False) - 1.0) * MASK_BIG)
    out_c = _attn_call(
        _fox_kernel, "forgetting", qkv3, ATTN_TILE, 0,
        [pl.BlockSpec((None, ATTN_TILE, LANES), lambda b, p, i: (b, i, 0)),
         pl.BlockSpec((None, None, HEADS_PER_BLOCK, seq), lambda b, p, i: (b, p, 0, 0)),
         _resident(causal.shape)],
        [c, c_t, causal], bound_row=_score_bound_row(g_q_c[0], g_k_c[0]))
    table = jnp.asarray(_dilated_bias_table(ATTN_TILE, DILATED_CHUNK))
    dil_bound = _score_bound_row(g_q_d[0], g_k_d[0])

    def dilated_dense():
        return _attn_call(_dilated_kernel, "dilated", qkv3, ATTN_TILE, 1, [_resident(table.shape)],
                          [table], bound_row=dil_bound)

    if seq % DILATED_SLAB == 0 and seq >= 2 * DILATED_SLAB:
        out_d = lax.cond(dil_bound[0, 0] <= SHIFT_LIMIT,
                         lambda: _dilated_classes(dil.reshape(b, seq, 3 * GROUP_WIDTH), dil_bound),
                         dilated_dense)
    else:
        out_d = dilated_dense()
    x2 = _mix_ffn(x2, out_c.reshape(n, GROUP_WIDTH), out_d.reshape(n, GROUP_WIDTH), bf(w_out[1]),
                  row(norm_ffn2[1]), bf(ffn2_w_gate[1]), bf(ffn2_w_up[1]), bf(ffn2_w_down[1]))
    return x2.reshape(b, seq, d)
```

```python
import functools

import numpy as np
import jax
import jax.numpy as jnp
from jax import lax
from jax.experimental import pallas as pl
from jax.experimental.pallas import tpu as pltpu

F32 = jnp.float32
BF16 = jnp.bfloat16

D_MODEL = 1024
HEAD_DIM = 64
D_MIX = 1024
D_FF = 2816
GROUP_HEADS = 8
GROUP_WIDTH = GROUP_HEADS * HEAD_DIM
LANES = 128
HEADS_PER_BLOCK = LANES // HEAD_DIM
N_PAIRS = GROUP_WIDTH // LANES
RMS_EPS = 1e-6
ROPE_THETA = 10000.0
FFN_RES_WEIGHT = 0.5
ATTN_SCALE = HEAD_DIM ** -0.5
MOBA_BLOCK = 256
MOBA_TOPK = 3
DILATED_BRANCHES = ((128, 1), (512, 4), (2048, 16))
SKIP_LOG = 88.0
VMEM_LIMIT_BYTES = 56 * 1024 * 1024

TOKEN_TILE = 512
FF_CHUNK = 256
PROJ_CHUNK = 256
SB_TILE = 256
ATTN_TILE = 256
MOBA_CHUNK = 2 * MOBA_BLOCK
MOBA_STEP_BLOCKS = 2
DILATED_CHUNK = 2 * ATTN_TILE
CLASS_TILE = 128
DILATED_SLAB = 16 * CLASS_TILE
MASK_BIG = 1e30
SHIFT_LIMIT = 40.0
CUMSUM_TILE = 256

_NT = (((1,), (1,)), ((), ()))


def _params(n_grid):
    return pltpu.CompilerParams(
        dimension_semantics=("arbitrary",) * n_grid,
        vmem_limit_bytes=VMEM_LIMIT_BYTES)


def _resident(shape):
    nd = len(shape)
    return pl.BlockSpec(shape, lambda *_: (0,) * nd, pipeline_mode=pl.Buffered(1))


def _dot(a, b):
    return jnp.dot(a, b, preferred_element_type=F32)


def _dot_nt(a, b):
    return lax.dot_general(a, b, _NT, preferred_element_type=F32)


def _split2(x):
    hi = x.astype(BF16)
    lo = (x - hi.astype(F32)).astype(BF16)
    return hi, lo


def _rms_rows(x, gain):
    return x * lax.rsqrt(jnp.mean(x * x, axis=-1, keepdims=True) + RMS_EPS) * gain


def _swiglu_residual(x, g_ref, wg_ref, wu_ref, wd_ref, h_ref):
    xn = _rms_rows(x, g_ref[...]).astype(BF16)
    for c in range(D_FF // FF_CHUNK):
        sl = slice(c * FF_CHUNK, (c + 1) * FF_CHUNK)
        gate = _dot(xn, wg_ref[:, sl])
        up = _dot(xn, wu_ref[:, sl])
        h_ref[:, sl] = (gate * jax.nn.sigmoid(gate) * up).astype(BF16)
    return x + FFN_RES_WEIGHT * _dot(h_ref[...], wd_ref[...])


def _ffn_kernel(x_ref, g_ref, wg_ref, wu_ref, wd_ref, o_ref, h_ref):
    o_ref[...] = _swiglu_residual(x_ref[...], g_ref, wg_ref, wu_ref, wd_ref, h_ref)


def _mix_ffn_kernel(x_ref, a_ref, b_ref, wo_ref, g_ref, wg_ref, wu_ref, wd_ref, o_ref, h_ref):
    x = (x_ref[...] + _dot(a_ref[...], wo_ref[0:GROUP_WIDTH, :])
         + _dot(b_ref[...], wo_ref[GROUP_WIDTH:D_MIX, :]))
    o_ref[...] = _swiglu_residual(x, g_ref, wg_ref, wu_ref, wd_ref, h_ref)


def _ffn_specs():
    return [pl.BlockSpec((1, D_MODEL), lambda i: (0, 0)),
            _resident((D_MODEL, D_FF)), _resident((D_MODEL, D_FF)), _resident((D_FF, D_MODEL))]


def _ffn(x2, gain, wg, wu, wd):
    n = x2.shape[0]
    row = pl.BlockSpec((TOKEN_TILE, D_MODEL), lambda i: (i, 0))
    return pl.pallas_call(
        _ffn_kernel,
        grid=(n // TOKEN_TILE,),
        in_specs=[row] + _ffn_specs(),
        out_specs=row,
        out_shape=jax.ShapeDtypeStruct((n, D_MODEL), F32),
        scratch_shapes=[pltpu.VMEM((TOKEN_TILE, D_FF), BF16)],
        compiler_params=_params(1),
        name="ffn",
    )(x2, gain, wg, wu, wd)


def _mix_ffn(x2, oa, ob, wo, gain, wg, wu, wd):
    n = x2.shape[0]
    row = pl.BlockSpec((TOKEN_TILE, D_MODEL), lambda i: (i, 0))
    half = pl.BlockSpec((TOKEN_TILE, GROUP_WIDTH), lambda i: (i, 0))
    return pl.pallas_call(
        _mix_ffn_kernel,
        grid=(n // TOKEN_TILE,),
        in_specs=[row, half, half, _resident((D_MIX, D_MODEL))] + _ffn_specs(),
        out_specs=row,
        out_shape=jax.ShapeDtypeStruct((n, D_MODEL), F32),
        scratch_shapes=[pltpu.VMEM((TOKEN_TILE, D_FF), BF16)],
        compiler_params=_params(1),
        name="mix_ffn",
    )(x2, oa, ob, wo, gain, wg, wu, wd)


def _head_sum_matrix():
    r = lax.broadcasted_iota(jnp.int32, (LANES, LANES), 0) // HEAD_DIM
    c = lax.broadcasted_iota(jnp.int32, (LANES, LANES), 1) // HEAD_DIM
    return jnp.where(r == c, 1.0, 0.0).astype(BF16)


def _head_rms(y, gain, hsum):
    ss = _dot((y * y).astype(BF16), hsum)
    return y * lax.rsqrt(ss * (1.0 / HEAD_DIM) + RMS_EPS) * gain


def _rope(y, cos, sin_signed, first_half):
    half = HEAD_DIM // 2
    partner = jnp.where(first_half, pltpu.roll(y, LANES - half, 1), pltpu.roll(y, half, 1))
    return y * cos + partner * sin_signed


def _proj_blocks(xn, w_ref, n_blocks):
    per = PROJ_CHUNK // LANES
    for c in range(0, n_blocks, per):
        width = min(per, n_blocks - c) * LANES
        y = _dot(xn, w_ref[:, c * LANES:c * LANES + width])
        for s in range(width // LANES):
            yield c + s, y[:, s * LANES:(s + 1) * LANES]


def _first_half_mask(rows):
    lane = lax.broadcasted_iota(jnp.int32, (rows, LANES), 1)
    return (lane % HEAD_DIM) < (HEAD_DIM // 2)


def _proj_ab_kernel(x_ref, g_ref, w_ref, gq_ref, gk_ref, cos_ref, sin_ref,
                    qkv_ref, qf_ref, km_ref):
    xn = _rms_rows(x_ref[...], g_ref[...]).astype(BF16)
    hsum = _head_sum_matrix()
    cos, sin = cos_ref[...], sin_ref[...]
    first_half = _first_half_mask(TOKEN_TILE)
    blocks_per_kind = D_MIX // LANES
    for cb, y in _proj_blocks(xn, w_ref, 3 * blocks_per_kind):
        kind, pp = divmod(cb, blocks_per_kind)
        if kind < 2 and pp >= N_PAIRS:
            gain = (gq_ref if kind == 0 else gk_ref)[...]
            y = _rope(_head_rms(y, gain, hsum), cos, sin, first_half)
            sl = slice((pp - N_PAIRS) * LANES, (pp - N_PAIRS + 1) * LANES)
            if kind == 0:
                qf_ref[:, sl] = y
            else:
                for jb in range(TOKEN_TILE // MOBA_BLOCK):
                    rows = y[jb * MOBA_BLOCK:(jb + 1) * MOBA_BLOCK, :]
                    km_ref[jb, :, sl] = jnp.mean(rows, axis=0, keepdims=True)
        if kind == 0:
            y = y * ATTN_SCALE
        qkv_ref[:, cb * LANES:(cb + 1) * LANES] = y.astype(BF16)


def _proj_cd_kernel(x_ref, g_ref, w_ref, gqc_ref, gkc_ref, gqd_ref, gkd_ref, bf_ref,
                    cos_ref, sin_ref, qkv_ref, lf_ref, dil_ref):
    xn = _rms_rows(x_ref[...], g_ref[...]).astype(BF16)
    hsum = _head_sum_matrix()
    cos, sin = cos_ref[...], sin_ref[...]
    first_half = _first_half_mask(TOKEN_TILE)
    blocks_per_kind = D_MIX // LANES
    for cb, y in _proj_blocks(xn, w_ref, 3 * blocks_per_kind + 1):
        kind, pp = divmod(cb, blocks_per_kind)
        if kind == 3:
            t = y + bf_ref[...]
            lf_ref[...] = jnp.minimum(t, 0.0) - jnp.log1p(jnp.exp(-jnp.abs(t)))
            continue
        if kind < 2:
            if pp < N_PAIRS:
                y = _head_rms(y, (gqc_ref if kind == 0 else gkc_ref)[...], hsum)
            else:
                y = _head_rms(y, (gqd_ref if kind == 0 else gkd_ref)[...], hsum)
                y = _rope(y, cos, sin, first_half)
        if kind == 0:
            y = y * ATTN_SCALE
        qkv_ref[:, cb * LANES:(cb + 1) * LANES] = y.astype(BF16)
        if pp >= N_PAIRS:
            col = (kind * N_PAIRS + pp - N_PAIRS) * LANES
            dil_ref[:, col:col + LANES] = y


def _proj_common_specs(seq, w_cols):
    row = pl.BlockSpec((TOKEN_TILE, D_MODEL), lambda i: (i, 0))
    tiles_per_seq = seq // TOKEN_TILE
    table = pl.BlockSpec((TOKEN_TILE, LANES), lambda i: (i % tiles_per_seq, 0))
    lane_vec = pl.BlockSpec((1, LANES), lambda i: (0, 0))
    gain = pl.BlockSpec((1, D_MODEL), lambda i: (0, 0))
    return row, gain, _resident((D_MODEL, w_cols)), lane_vec, table


def _proj_ab(x2, seq, gain, w, gq, gk, cos, sin):
    n = x2.shape[0]
    row, gspec, wspec, lane_vec, table = _proj_common_specs(seq, 3 * D_MIX)
    blocks = TOKEN_TILE // MOBA_BLOCK
    return pl.pallas_call(
        _proj_ab_kernel,
        grid=(n // TOKEN_TILE,),
        in_specs=[row, gspec, wspec, lane_vec, lane_vec, table, table],
        out_specs=[pl.BlockSpec((TOKEN_TILE, 3 * D_MIX), lambda i: (i, 0)),
                   pl.BlockSpec((TOKEN_TILE, GROUP_WIDTH), lambda i: (i, 0)),
                   pl.BlockSpec((blocks, 1, GROUP_WIDTH), lambda i: (i, 0, 0))],
        out_shape=[jax.ShapeDtypeStruct((n, 3 * D_MIX), BF16),
                   jax.ShapeDtypeStruct((n, GROUP_WIDTH), F32),
                   jax.ShapeDtypeStruct((n // MOBA_BLOCK, 1, GROUP_WIDTH), F32)],
        compiler_params=_params(1),
        name="proj_ab",
    )(x2, gain, w, gq, gk, cos, sin)


def _proj_cd(x2, seq, gain, w, gqc, gkc, gqd, gkd, bf, cos, sin):
    n = x2.shape[0]
    row, gspec, wspec, lane_vec, table = _proj_common_specs(seq, 3 * D_MIX + LANES)
    return pl.pallas_call(
        _proj_cd_kernel,
        grid=(n // TOKEN_TILE,),
        in_specs=[row, gspec, wspec, lane_vec, lane_vec, lane_vec, lane_vec, lane_vec, table, table],
        out_specs=[pl.BlockSpec((TOKEN_TILE, 3 * D_MIX), lambda i: (i, 0)),
                   pl.BlockSpec((TOKEN_TILE, LANES), lambda i: (i, 0)),
                   pl.BlockSpec((TOKEN_TILE, 3 * GROUP_WIDTH), lambda i: (i, 0))],
        out_shape=[jax.ShapeDtypeStruct((n, 3 * D_MIX), BF16),
                   jax.ShapeDtypeStruct((n, LANES), F32),
                   jax.ShapeDtypeStruct((n, 3 * GROUP_WIDTH), F32)],
        compiler_params=_params(1),
        name="proj_cd",
    )(x2, gain, w, gqc, gkc, gqd, gkd, bf, cos, sin)


def _cumsum_kernel(lf_ref, c_ref, carry_ref):
    @pl.when(pl.program_id(1) == 0)
    def _():
        carry_ref[...] = jnp.zeros_like(carry_ref)

    t = CUMSUM_TILE
    r = lax.broadcasted_iota(jnp.int32, (t, t), 0)
    c = lax.broadcasted_iota(jnp.int32, (t, t), 1)
    lower = jnp.where(c <= r, 1.0, 0.0).astype(BF16)
    lf = lf_ref[...]
    hi = lf.astype(BF16)
    mid, lo = _split2(lf - hi.astype(F32))
    out = _dot(lower, hi) + _dot(lower, mid) + _dot(lower, lo) + carry_ref[0:1, :]
    c_ref[...] = out
    carry_ref[0:1, :] = out[t - 1:t, :]


def _cumsum(lf3):
    b, s, _ = lf3.shape
    blk = pl.BlockSpec((None, CUMSUM_TILE, LANES), lambda bi, i: (bi, i, 0))
    return pl.pallas_call(
        _cumsum_kernel,
        grid=(b, s // CUMSUM_TILE),
        in_specs=[blk],
        out_specs=blk,
        out_shape=jax.ShapeDtypeStruct(lf3.shape, F32),
        scratch_shapes=[pltpu.VMEM((8, LANES), F32)],
        compiler_params=_params(2),
        name="forget_cumsum",
    )(lf3)


def _pair_queries(q, rows):
    head = lax.broadcasted_iota(jnp.int32, (rows, LANES), 1) // HEAD_DIM
    return head, [jnp.where(head == h, q, jnp.zeros_like(q)) for h in range(HEADS_PER_BLOCK)]


def _softmax_step(s, m, l, acc, vb):
    m_new = jnp.maximum(m, jnp.max(s, axis=1, keepdims=True))
    alpha = jnp.exp(m - m_new)
    p = jnp.exp(s - m_new)
    l = alpha * l + jnp.sum(p, axis=1, keepdims=True)
    acc = alpha * acc + _dot(p.astype(BF16), vb)
    return m_new, l, acc


def _softmax_init(rows):
    return (jnp.full((rows, 1), -jnp.inf, F32), jnp.zeros((rows, 1), F32),
            jnp.zeros((rows, LANES), F32))


def _loop_grouped(n, body, init, groups=(2, 1)):
    done, state = 0, init
    for g in groups:
        def grouped(c, st, g=g, done=done):
            for u in range(g):
                st = body(done + g * c + u, st)
            return st
        count = (n - done) // g
        state = lax.fori_loop(0, count, grouped, state)
        done = done + count * g
    return state


def _causal_bias(t):
    row = lax.broadcasted_iota(jnp.int32, (t, t), 0)
    col = lax.broadcasted_iota(jnp.int32, (t, t), 1)
    return jnp.where(col <= row, 0.0, -MASK_BIG)


def _values_with_ones(vb):
    head = lax.broadcasted_iota(jnp.int32, vb.shape, 1) // HEAD_DIM
    return [jnp.where(head == h, vb, jnp.ones_like(vb)) for h in range(HEADS_PER_BLOCK)]


def _normalize_heads(head, accs):
    outs = [acc / pltpu.roll(acc, HEAD_DIM, 1) for acc in accs]
    return jnp.where(head == 0, outs[0], outs[1])


def _store_heads(o_ref, head, outs):
    o_ref[...] = jnp.where(head == 0, outs[0], outs[1]).astype(o_ref.dtype)


def _sb_kernel(q_ref, sfx_ref, mask_ref, k_ref, v_ref, o_ref):
    t = SB_TILE
    i = pl.program_id(2)
    head, qhs = _pair_queries(q_ref[...], t)
    sfx = sfx_ref[...]
    lanes_of = lambda x: jnp.concatenate([x] * (t // LANES), axis=1)

    def block_sums(log_keep):
        hi, lo = _split2(log_keep)
        return _dot(jnp.concatenate([hi, lo], axis=1), sfx)

    def log_gates(qh, kb):
        z = _dot_nt(qh, kb)
        soft = jnp.log(1.0 + jnp.exp(-jnp.abs(z)))
        return jnp.minimum(z, 0.0) - soft, jnp.minimum(-z, 0.0) - soft

    which = jnp.minimum(i, 1)
    start = pl.multiple_of(jnp.maximum(i - 1, 0) * t, t)
    strict = mask_ref[which, 0]
    strict_neg = mask_ref[which, 1]
    kb = k_ref[pl.ds(start, 2 * t), :]
    vb = v_ref[pl.ds(start, 2 * t), :]
    carry = (i - 2,)
    for qh in qhs:
        log_beta, log_keep = log_gates(qh, kb)
        log_keep = log_keep * strict
        late = block_sums(log_keep[:, t:])
        early = block_sums(log_keep[:, :t])
        between = jnp.concatenate([early[:, :t] + lanes_of(late[:, t:]), late[:, :t]], axis=1)
        w = jnp.exp(log_beta + between + strict_neg)
        carry += (late[:, t:] + early[:, t:], _dot(w.astype(BF16), vb))

    def step(j, run, acc, qh):
        start = pl.multiple_of(j * t, t)
        log_beta, log_keep = log_gates(qh, k_ref[pl.ds(start, t), :])
        sums = block_sums(log_keep)
        w = jnp.exp(log_beta + lanes_of(run) + sums[:, :t])
        return run + sums[:, t:], acc + _dot(w.astype(BF16), v_ref[pl.ds(start, t), :])

    def cond(carry):
        return (carry[0] >= 0) & (jnp.maximum(jnp.max(carry[1]), jnp.max(carry[3])) > -SKIP_LOG)

    def body(carry):
        out = (carry[0] - 1,)
        for h, qh in enumerate(qhs):
            out += step(carry[0], carry[1 + 2 * h], carry[2 + 2 * h], qh)
        return out

    carry = lax.while_loop(cond, body, carry)
    _store_heads(o_ref, head, [carry[2], carry[4]])


def _score_bound_row(g_q, g_k):
    bound = 1.05 * HEAD_DIM * ATTN_SCALE * jnp.max(jnp.abs(g_q)) * jnp.max(jnp.abs(g_k))
    return jnp.full((1, LANES), bound, F32)


def _moba_kernel(bound_ref, q_ref, qf_ref, km_ref, onehot_ref, k_ref, v_ref, o_ref):
    t, ch, nq = MOBA_BLOCK, MOBA_CHUNK, MOBA_STEP_BLOCKS
    rows = nq * t
    per = ch // t
    first_own = pl.program_id(2) * nq
    nkb = km_ref.shape[0]
    q, qf = q_ref[...], qf_ref[...]
    head, qhs = _pair_queries(q, rows)
    bound = bound_ref[...][:, 0:1]
    lane = lax.broadcasted_iota(jnp.int32, (rows, LANES), 1)
    km_hi, km_lo = _split2(km_ref[...])
    blk_t = lax.broadcasted_iota(jnp.int32, (nkb, rows), 0)
    blk_tf = blk_t.astype(F32)
    own_t = first_own + lax.broadcasted_iota(jnp.int32, (nkb, rows), 1) // t
    own_row = own_t[0:1, :]
    causal = _causal_bias(t)
    n_chunks = (first_own + nq - 1 + per - 1) // per

    blocked = []
    for h in range(HEADS_PER_BLOCK):
        q_hi, q_lo = _split2(jnp.where(head == h, qf, 0.0))
        gate = _dot_nt(km_hi, q_hi) + _dot_nt(km_lo, q_hi) + _dot_nt(km_hi, q_lo)
        gate = jnp.where(blk_t < own_t, gate, -jnp.inf)
        sel = jnp.zeros((nkb, rows), F32)
        for r in range(MOBA_TOPK):
            best = jnp.max(gate, axis=0, keepdims=True)
            first = jnp.min(jnp.where(gate == best, blk_tf, float(nkb)), axis=0, keepdims=True)
            hit = blk_tf == first
            sel = sel + jnp.where(hit, jnp.where(own_row > r, 1.0, 0.0), 0.0)
            gate = jnp.where(hit, -jnp.inf, gate)
        neg = jnp.concatenate([(sel - 1.0) * MASK_BIG, jnp.zeros((LANES - nkb, rows), F32)], axis=0)
        blocked.append(neg.T)

    def rhs_of(start, count):
        return jnp.concatenate([k_ref[pl.ds(start, count), :], onehot_ref[pl.ds(start, count), :]], axis=1)

    def own_blocks(fn):
        parts = [fn(slice(s * t, (s + 1) * t), pl.multiple_of((first_own + s) * t, t)) for s in range(nq)]
        return tuple(jnp.concatenate(p, axis=0) for p in zip(*parts))

    def shifted():
        lhs, lhs_own = [], []
        for h, qh in enumerate(qhs):
            extra = jnp.where(lane == LANES - 1, -bound, blocked[h])
            lhs.append(jnp.concatenate([qh, extra.astype(BF16)], axis=1))
            extra_own = jnp.where(lane == LANES - 1, -bound, 0.0)
            lhs_own.append(jnp.concatenate([qh, extra_own.astype(BF16)], axis=1))

        def own_pass(rs, start):
            rhs_own = rhs_of(start, t)
            v_owns = _values_with_ones(v_ref[pl.ds(start, t), :])
            return tuple(_dot(jnp.exp(_dot_nt(a[rs], rhs_own) + causal).astype(BF16), vh)
                         for a, vh in zip(lhs_own, v_owns))

        accs = own_blocks(own_pass)

        def body(c, accs):
            start = pl.multiple_of(c * ch, ch)
            rhs = rhs_of(start, ch)
            vhs = _values_with_ones(v_ref[pl.ds(start, ch), :])
            return tuple(acc + _dot(jnp.exp(_dot_nt(a, rhs)).astype(BF16), vh)
                         for a, vh, acc in zip(lhs, vhs, accs))

        accs = _loop_grouped(n_chunks, body, accs, groups=(2, 1))
        return _normalize_heads(head, accs)

    def running_max():
        lhs = [jnp.concatenate([qh, neg.astype(BF16)], axis=1) for qh, neg in zip(qhs, blocked)]

        def own_pass(rs, start):
            k_own, v_own = k_ref[pl.ds(start, t), :], v_ref[pl.ds(start, t), :]
            out = ()
            for qh in qhs:
                out += _softmax_step(_dot_nt(qh[rs], k_own) + causal, *_softmax_init(t), v_own)
            return out

        flat = own_blocks(own_pass)
        states = tuple(flat[3 * h:3 * h + 3] for h in range(HEADS_PER_BLOCK))

        def body(c, states):
            start = pl.multiple_of(c * ch, ch)
            rhs = rhs_of(start, ch)
            vb = v_ref[pl.ds(start, ch), :]
            return tuple(_softmax_step(_dot_nt(a, rhs), *st, vb) for a, st in zip(lhs, states))

        states = lax.fori_loop(0, n_chunks, body, states)
        outs = [acc / l for _, l, acc in states]
        return jnp.where(head == 0, outs[0], outs[1])

    out = lax.cond(jnp.max(bound) <= SHIFT_LIMIT, shifted, running_max)
    o_ref[...] = out.astype(o_ref.dtype)


def _fox_kernel(bound_ref, q_ref, c_ref, ct_ref, causal_ref, k_ref, v_ref, o_ref):
    t = ATTN_TILE
    pair = pl.program_id(1)
    i = pl.program_id(2)

    q = q_ref[...]
    head, qhs = _pair_queries(q, t)
    qk_bound = bound_ref[...][:, 0:1]
    c_blk = c_ref[...]
    lane = lax.broadcasted_iota(jnp.int32, (t, LANES), 1)
    c_qs = [jnp.sum(jnp.where(lane == HEADS_PER_BLOCK * pair + h, c_blk, 0.0), axis=1, keepdims=True)
            for h in range(HEADS_PER_BLOCK)]
    bounds = [qk_bound] * HEADS_PER_BLOCK

    def block_min_c(h, j):
        start = pl.multiple_of(jnp.maximum(j, 0) * t, t)
        return jnp.min(ct_ref[h:h + 1, pl.ds(start, t)])

    def shifted():
        lhs = [jnp.concatenate([qh, jnp.where(lane == LANES - 1, -bounds[h], 0.0).astype(BF16)], axis=1)
               for h, qh in enumerate(qhs)]
        ones_col = {rows: jnp.where(lax.broadcasted_iota(jnp.int32, (rows, LANES), 1) == LANES - 1,
                                    1.0, 0.0).astype(BF16) for rows in (t, 2 * t)}
        level = [jnp.max(c_qs[h] + 2.0 * bounds[h]) for h in range(HEADS_PER_BLOCK)]

        def attend(start, rows, bias, accs):
            rhs = jnp.concatenate([k_ref[pl.ds(start, rows), :], ones_col[rows]], axis=1)
            vhs = _values_with_ones(v_ref[pl.ds(start, rows), :])
            out = ()
            for h, a in enumerate(lhs):
                s = _dot_nt(a, rhs) + (c_qs[h] - ct_ref[h:h + 1, pl.ds(start, rows)])
                p = jnp.exp(s if bias is None else s + bias)
                out += (accs[h] + _dot(p.astype(BF16), vhs[h]),)
            return out

        zeros = jnp.zeros((t, LANES), F32)
        start = pl.multiple_of(jnp.maximum(i - 1, 0) * t, t)
        carry = (i - 2,) + attend(start, 2 * t, causal_ref[jnp.minimum(i, 1)], (zeros,) * 2)

        def cond(carry):
            j = carry[0]
            margin = [level[h] - block_min_c(h, j) for h in range(HEADS_PER_BLOCK)]
            return (j >= 0) & (jnp.maximum(margin[0], margin[1]) > -SKIP_LOG)

        def body(carry):
            start = pl.multiple_of(carry[0] * t, t)
            return (carry[0] - 1,) + attend(start, t, None, carry[1:])

        carry = lax.while_loop(cond, body, carry)
        return _normalize_heads(head, carry[1:])

    def running_max():
        reach =[bounds[h] + c_qs[h] for h in range(HEADS_PER_BLOCK)]

        def step(j, states, extra):
            start = pl.multiple_of(j * t, t)
            kb = k_ref[pl.ds(start, t), :]
            vb = v_ref[pl.ds(start, t), :]
            out = ()
            for h, qh in enumerate(qhs):
                s = _dot_nt(qh, kb) + (c_qs[h] - ct_ref[h:h + 1, pl.ds(start, t)])
                if extra is not None:
                    s = s + extra
                out += _softmax_step(s, *states[3 * h:3 * h + 3], vb)
            return out

        carry = (i - 1,) + step(i, _softmax_init(t) * HEADS_PER_BLOCK, _causal_bias(t))

        def cond(carry):
            j = carry[0]
            margin = [jnp.max(reach[h] - carry[1 + 3 * h]) - block_min_c(h, j)
                      for h in range(HEADS_PER_BLOCK)]
            return (j >= 0) & (jnp.maximum(margin[0], margin[1]) > -SKIP_LOG)

        def body(carry):
            return (carry[0] - 1,) + step(carry[0], carry[1:], None)

        carry = lax.while_loop(cond, body, carry)
        outs = [carry[3 + 3 * h] / carry[2 + 3 * h] for h in range(HEADS_PER_BLOCK)]
        return jnp.where(head == 0, outs[0], outs[1])

    out = lax.cond(jnp.max(qk_bound) <= SHIFT_LIMIT, shifted, running_max)
    o_ref[...] = out.astype(o_ref.dtype)


def _dilated_kernel(bound_ref, q_ref, tbl_ref, k_ref, v_ref, o_ref):
    t, ch = ATTN_TILE, DILATED_CHUNK
    i = pl.program_id(2)
    q = q_ref[...]
    head, qhs = _pair_queries(q, t)
    bound = bound_ref[...][:, 0:1]
    half = i // 2
    parity = i - 2 * half
    n_chunks = jnp.minimum(half + 1, tbl_ref.shape[0] // 2)

    def chunk(c):
        start = pl.multiple_of((half - c) * ch, ch)
        return k_ref[pl.ds(start, ch), :], v_ref[pl.ds(start, ch), :], tbl_ref[parity + 2 * c]

    def shifted():
        lane = lax.broadcasted_iota(jnp.int32, (t, LANES), 1)
        lhs = [jnp.concatenate(
            [qh, jnp.where(lane == LANES - 1, -bound, 0.0).astype(BF16)],
            axis=1) for h, qh in enumerate(qhs)]
        ones_col = jnp.where(lax.broadcasted_iota(jnp.int32, (ch, LANES), 1) == LANES - 1, 1.0, 0.0).astype(BF16)

        def body(c, states):
            kb, vb, bias = chunk(c)
            rhs = jnp.concatenate([kb, ones_col], axis=1)
            out = ()
            for a, (l, acc) in zip(lhs, states):
                p = jnp.exp(_dot_nt(a, rhs) + bias)
                for c0 in range(0, ch, LANES):
                    l = l + p[:, c0:c0 + LANES]
                out += ((l, acc + _dot(p.astype(BF16), vb)),)
            return out

        zeros = jnp.zeros((t, LANES), F32)
        states = _loop_grouped(n_chunks, body, ((zeros, zeros),) * HEADS_PER_BLOCK)
        outs = [acc / jnp.sum(l, axis=1, keepdims=True) for l, acc in states]
        return jnp.where(head == 0, outs[0], outs[1])

    def running_max():
        def body(c, states):
            kb, vb, bias = chunk(c)
            return tuple(_softmax_step(_dot_nt(qh, kb) + bias, *st, vb) for qh, st in zip(qhs, states))

        states = lax.fori_loop(0, n_chunks, body, (_softmax_init(t),) * HEADS_PER_BLOCK)
        outs = [acc / l for _, l, acc in states]
        return jnp.where(head == 0, outs[0], outs[1])

    out = lax.cond(jnp.max(bound) <= SHIFT_LIMIT, shifted, running_max)
    o_ref[...] = out.astype(o_ref.dtype)


def _dilated_class_kernel(shift_ref, band_ref, q_ref, k_ref, v_ref, o_ref, tot_ref):
    ct = CLASS_TILE
    slab = pl.program_id(2)
    head = lax.broadcasted_iota(jnp.int32, (ct, LANES), 1) // HEAD_DIM
    shift = shift_ref[...][:, 0:1]
    band_first = band_ref[0] - shift
    band = band_ref[1] - shift
    band_edge = jnp.where(slab == 0, band_first, band)

    def rows(ref, start, count, d):
        return ref[pl.ds(start, count), :] if d == 1 else ref[pl.ds(start, count, stride=d), :]

    for branch, (window, d) in enumerate(DILATED_BRANCHES):
        assert window == ct * d and DILATED_SLAB % (ct * d) == 0
        for r in range(d):
            for u in range(DILATED_SLAB // (ct * d)):
                row0 = r + d * ct * u
                qb = rows(q_ref, row0, ct, d).astype(BF16)
                key0 = slab * DILATED_SLAB + (row0 - d * ct)
                start = jnp.maximum(key0, r) if u == 0 else key0
                kb = rows(k_ref, start, 2 * ct, d).astype(BF16)
                vhs = _values_with_ones(rows(v_ref, start, 2 * ct, d).astype(BF16))
                for h in range(HEADS_PER_BLOCK):
                    qh = jnp.where(head == h, qb, jnp.zeros_like(qb))
                    p = jnp.exp(_dot_nt(qh, kb) + (band_edge if u == 0 else band))
                    acc = _dot(p.astype(BF16), vhs[h])
                    if branch == 0:
                        tot_ref[h, pl.ds(row0, ct), :] = acc
                    elif d == 1:
                        tot_ref[h, pl.ds(row0, ct), :] += acc
                    else:
                        tot_ref[h, pl.ds(row0, ct, stride=d), :] += acc

    slab_head = lax.broadcasted_iota(jnp.int32, (DILATED_SLAB, LANES), 1) // HEAD_DIM
    out = _normalize_heads(slab_head, [tot_ref[h] for h in range(HEADS_PER_BLOCK)])
    o_ref[...] = out.astype(o_ref.dtype)


def _dilated_classes(dil3, shift_row):
    b, seq, _ = dil3.shape
    ct = CLASS_TILE
    r = np.arange(ct)[:, None]
    c = np.arange(2 * ct)[None, :]
    band = np.stack([c <= r, (c >= r) & (c <= r + ct)]).astype(np.float32)
    band = jnp.asarray((band - 1.0) * MASK_BIG)
    return pl.pallas_call(
        _dilated_class_kernel,
        grid=(b, N_PAIRS, seq // DILATED_SLAB),
        in_specs=[pl.BlockSpec((1, LANES), lambda b, p, i: (0, 0)),
                  _resident(band.shape),
                  pl.BlockSpec((None, DILATED_SLAB, LANES), lambda b, p, i: (b, i, p)),
                  pl.BlockSpec((None, seq, LANES), lambda b, p, i: (b, 0, N_PAIRS + p)),
                  pl.BlockSpec((None, seq, LANES), lambda b, p, i: (b, 0, 2 * N_PAIRS + p))],
        out_specs=pl.BlockSpec((None, DILATED_SLAB, LANES), lambda b, p, i: (b, i, p)),
        out_shape=jax.ShapeDtypeStruct((b, seq, GROUP_WIDTH), BF16),
        scratch_shapes=[pltpu.VMEM((HEADS_PER_BLOCK, DILATED_SLAB, LANES), F32)],
        compiler_params=_params(3),
        name="dilated_classes",
    )(shift_row, band, dil3, dil3, dil3)


def _qkv_specs(seq, tq, group):
    blocks_per_kind = D_MIX // LANES
    off = group * N_PAIRS
    q = pl.BlockSpec((None, tq, LANES), lambda b, p, i: (b, i, off + p))
    k = pl.BlockSpec((None, seq, LANES), lambda b, p, i: (b, 0, blocks_per_kind + off + p))
    v = pl.BlockSpec((None, seq, LANES), lambda b, p, i: (b, 0, 2 * blocks_per_kind + off + p))
    return q, k, v


def _attn_call(kernel_fn, name, qkv3, tq, group, extra_specs, extra_args, bound_row=None):
    b, seq, _ = qkv3.shape
    q, k, v = _qkv_specs(seq, tq, group)
    lead_specs = [] if bound_row is None else [pl.BlockSpec((1, LANES), lambda b, p, i: (0, 0))]
    lead_args = [] if bound_row is None else [bound_row]
    return pl.pallas_call(
        kernel_fn,
        grid=(b, N_PAIRS, seq // tq),
        in_specs=lead_specs + [q] + list(extra_specs) + [k, v],
        out_specs=pl.BlockSpec((None, tq, LANES), lambda b, p, i: (b, i, p)),
        out_shape=jax.ShapeDtypeStruct((b, seq, GROUP_WIDTH), BF16),
        compiler_params=_params(3),
        name=name,
    )(*lead_args, qkv3, *extra_args, qkv3, qkv3)


def _dilated_bias_table(t, ch):
    reach = max(w for w, _ in DILATED_BRANCHES)
    n = 2 * ((reach // t + 1 + ch // t) // 2)
    r = np.arange(t)[:, None]
    c = np.arange(ch)[None, :]
    out = np.zeros((n, t, ch), np.float32)
    for e in range(n):
        dist = e * t + r - c
        count = np.zeros((t, ch), np.float32)
        for window, dil in DILATED_BRANCHES:
            count += (dist >= 0) & (dist <= window) & (dist % dil == 0)
        out[e] = np.where(count > 0, np.log(np.maximum(count, 1.0)), -MASK_BIG)
    return out


def _suffix_matrix(t):
    j = np.arange(2 * t)[:, None] % t
    s = np.arange(t + LANES)[None, :]
    return jnp.asarray(((s >= t) | (j > s)).astype(np.float32), dtype=BF16)


def _first_step_masks(t, strict):
    r = np.arange(t)[:, None]
    c = np.arange(2 * t)[None, :]
    return np.stack([(c < r + d) if strict else (c <= r + d) for d in (0, t)]).astype(np.float32)


def _block_onehot(seq):
    assert seq // MOBA_BLOCK < LANES
    out = np.zeros((seq, LANES), np.float32)
    out[np.arange(seq), np.arange(seq) // MOBA_BLOCK] = 1.0
    out[:, LANES - 1] = 1.0
    return jnp.asarray(out, dtype=BF16)


def _rope_tables(seq):
    inv = 1.0 / (ROPE_THETA ** (jnp.arange(0, HEAD_DIM, 2, dtype=F32) / HEAD_DIM))
    ang = jnp.arange(seq, dtype=F32)[:, None] * inv[None, :]
    cos, sin = jnp.cos(ang), jnp.sin(ang)
    reps = LANES // HEAD_DIM
    cos_l = jnp.tile(jnp.concatenate([cos, cos], axis=1), (1, reps))
    sin_l = jnp.tile(jnp.concatenate([-sin, sin], axis=1), (1, reps))
    return cos_l, sin_l


def _lane_gain(g):
    return jnp.tile(g.astype(F32), LANES // HEAD_DIM)[None, :]


def kernel(x, norm_ffn1, ffn1_w_gate, ffn1_w_up, ffn1_w_down, norm_mix, w_in_ab, g_q_b, g_k_b,
           w_in_cd, b_f, g_q_c, g_k_c, g_q_d, g_k_d, w_out, norm_ffn2, ffn2_w_gate, ffn2_w_up,
           ffn2_w_down):
    b, seq, d = x.shape
    n = b * seq
    assert d == D_MODEL and n % TOKEN_TILE == 0 and seq % TOKEN_TILE == 0
    assert seq % MOBA_CHUNK == 0 and seq % SB_TILE == 0 and seq % DILATED_CHUNK == 0
    cos, sin = _rope_tables(seq)
    bf = lambda w: w.astype(BF16)
    row = lambda g: g.astype(F32)[None, :]
    x2 = x.reshape(n, d)

    x2 = _ffn(x2, row(norm_ffn1[0]), bf(ffn1_w_gate[0]), bf(ffn1_w_up[0]), bf(ffn1_w_down[0]))
    qkv, qf, km = _proj_ab(x2, seq, row(norm_mix[0]), bf(w_in_ab[0]),
                           _lane_gain(g_q_b[0]), _lane_gain(g_k_b[0]), cos, sin)
    qkv3 = qkv.reshape(b, seq, 3 * D_MIX)
    nkb = seq // MOBA_BLOCK
    sfx =_suffix_matrix(SB_TILE)
    strict = _first_step_masks(SB_TILE, True)
    sb_masks = jnp.asarray(np.stack([strict, (strict - 1.0) * MASK_BIG], axis=1))
    out_a = _attn_call(_sb_kernel, "stick_breaking", qkv3, SB_TILE, 0,
                       [_resident(sfx.shape), _resident(sb_masks.shape)], [sfx, sb_masks])
    out_b = _attn_call(
        _moba_kernel, "moba", qkv3, MOBA_STEP_BLOCKS * MOBA_BLOCK, 1,
        [pl.BlockSpec((None, MOBA_STEP_BLOCKS * MOBA_BLOCK, LANES), lambda b, p, i: (b, i, p)),
         pl.BlockSpec((None, nkb, LANES), lambda b, p, i: (b, 0, p)),
         _resident((seq, LANES))],
        [qf.reshape(b, seq, GROUP_WIDTH), km.reshape(b, nkb, GROUP_WIDTH), _block_onehot(seq)],
        bound_row=_score_bound_row(g_q_b[0], g_k_b[0]))
    x2 = _mix_ffn(x2, out_a.reshape(n, GROUP_WIDTH), out_b.reshape(n, GROUP_WIDTH), bf(w_out[0]),
                  row(norm_ffn2[0]), bf(ffn2_w_gate[0]), bf(ffn2_w_up[0]), bf(ffn2_w_down[0]))

    x2 = _ffn(x2, row(norm_ffn1[1]), bf(ffn1_w_gate[1]), bf(ffn1_w_up[1]), bf(ffn1_w_down[1]))
    w_cd = jnp.pad(bf(w_in_cd[0]), ((0, 0), (0, LANES - GROUP_HEADS)))
    b_f_l = jnp.pad(b_f[0].astype(F32), (0, LANES - GROUP_HEADS))[None, :]
    qkv, lf, dil = _proj_cd(
        x2, seq, row(norm_mix[1]), w_cd, _lane_gain(g_q_c[0]), _lane_gain(g_k_c[0]),
        _lane_gain(g_q_d[0]), _lane_gain(g_k_d[0]), b_f_l, cos, sin)
    qkv3 = qkv.reshape(b, seq, 3 * D_MIX)
    c = _cumsum(lf.reshape(b, seq, LANES))
    c_t = jnp.swapaxes(c[:, :, :GROUP_HEADS], 1, 2).reshape(b, N_PAIRS, HEADS_PER_BLOCK, seq)
    causal = jnp.asarray((_first_step_masks(ATTN_TILE, False) - 1.0) * MASK_BIG)
    out_c = _attn_call(
        _fox_kernel, "forgetting", qkv3, ATTN_TILE, 0,
        [pl.BlockSpec((None, ATTN_TILE, LANES), lambda b, p, i: (b, i, 0)),
         pl.BlockSpec((None, None, HEADS_PER_BLOCK, seq), lambda b, p, i: (b, p, 0, 0)),
         _resident(causal.shape)],
        [c, c_t, causal], bound_row=_score_bound_row(g_q_c[0], g_k_c[0]))
    table = jnp.asarray(_dilated_bias_table(ATTN_TILE, DILATED_CHUNK))
    dil_bound = _score_bound_row(g_q_d[0], g_k_d[0])

    def dilated_dense():
        return _attn_call(_dilated_kernel, "dilated", qkv3, ATTN_TILE, 1, [_resident(table.shape)],
                          [table], bound_row=dil_bound)

    if seq % DILATED_SLAB == 0 and seq >= 2 * DILATED_SLAB:
        out_d = lax.cond(dil_bound[0, 0] <= SHIFT_LIMIT,
                         lambda: _dilated_classes(dil.reshape(b, seq, 3 * GROUP_WIDTH), dil_bound),
                         dilated_dense)
    else:
        out_d = dilated_dense()
    x2 = _mix_ffn(x2, out_c.reshape(n, GROUP_WIDTH), out_d.reshape(n, GROUP_WIDTH), bf(w_out[1]),
                  row(norm_ffn2[1]), bf(ffn2_w_gate[1]), bf(ffn2_w_up[1]), bf(ffn2_w_down[1]))
    return x2.reshape(b, seq, d)
```

```python
import functools

import numpy as np
import jax
import jax.numpy as jnp
from jax import lax
from jax.experimental import pallas as pl
from jax.experimental.pallas import tpu as pltpu

F32 = jnp.float32
BF16 = jnp.bfloat16

D_MODEL = 1024
HEAD_DIM = 64
D_MIX = 1024
D_FF = 2816
GROUP_HEADS = 8
GROUP_WIDTH = GROUP_HEADS * HEAD_DIM
LANES = 128
HEADS_PER_BLOCK = LANES // HEAD_DIM
N_PAIRS = GROUP_WIDTH // LANES
RMS_EPS = 1e-6
ROPE_THETA = 10000.0
FFN_RES_WEIGHT = 0.5
ATTN_SCALE = HEAD_DIM ** -0.5
MOBA_BLOCK = 256
MOBA_TOPK = 3
DILATED_BRANCHES = ((128, 1), (512, 4), (2048, 16))
SKIP_LOG = 88.0
VMEM_LIMIT_BYTES = 56 * 1024 * 1024

TOKEN_TILE = 512
FF_CHUNK = 256
PROJ_CHUNK = 256
SB_TILE = 256
ATTN_TILE = 256
MOBA_CHUNK = 2 * MOBA_BLOCK
MOBA_STEP_BLOCKS = 4
ATTN_STEP_BLOCKS = 2
DILATED_CHUNK = 2 * ATTN_TILE
CLASS_TILE = 128
DILATED_SLAB = 16 * CLASS_TILE
MASK_BIG = 1e30
SHIFT_LIMIT = 40.0
CUMSUM_TILE = 256

_NT = (((1,), (1,)), ((), ()))


def _params(n_grid):
    return pltpu.CompilerParams(
        dimension_semantics=("arbitrary",) * n_grid,
        vmem_limit_bytes=VMEM_LIMIT_BYTES)


def _resident(shape):
    nd = len(shape)
    return pl.BlockSpec(shape, lambda *_: (0,) * nd, pipeline_mode=pl.Buffered(1))


def _dot(a, b):
    return jnp.dot(a, b, preferred_element_type=F32)


def _dot_nt(a, b):
    return lax.dot_general(a, b, _NT, preferred_element_type=F32)


def _split2(x):
    hi = x.astype(BF16)
    lo = (x - hi.astype(F32)).astype(BF16)
    return hi, lo


def _rms_rows(x, gain):
    return x * lax.rsqrt(jnp.mean(x * x, axis=-1, keepdims=True) + RMS_EPS) * gain


def _swiglu_residual(x, g_ref, wg_ref, wu_ref, wd_ref, h_ref):
    xn = _rms_rows(x, g_ref[...]).astype(BF16)
    for c in range(D_FF // FF_CHUNK):
        sl = slice(c * FF_CHUNK, (c + 1) * FF_CHUNK)
        gate = _dot(xn, wg_ref[:, sl])
        up = _dot(xn, wu_ref[:, sl])
        h_ref[:, sl] = (gate * jax.nn.sigmoid(gate) * up).astype(BF16)
    return x + FFN_RES_WEIGHT * _dot(h_ref[...], wd_ref[...])


def _ffn_kernel(x_ref, g_ref, wg_ref, wu_ref, wd_ref, o_ref, h_ref):
    o_ref[...] = _swiglu_residual(x_ref[...], g_ref, wg_ref, wu_ref, wd_ref, h_ref)


def _mix_ffn_kernel(x_ref, a_ref, b_ref, wo_ref, g_ref, wg_ref, wu_ref, wd_ref, o_ref, h_ref):
    x = (x_ref[...] + _dot(a_ref[...], wo_ref[0:GROUP_WIDTH, :])
         + _dot(b_ref[...], wo_ref[GROUP_WIDTH:D_MIX, :]))
    o_ref[...] = _swiglu_residual(x, g_ref, wg_ref, wu_ref, wd_ref, h_ref)


def _ffn_specs():
    return [pl.BlockSpec((1, D_MODEL), lambda i: (0, 0)),
            _resident((D_MODEL, D_FF)), _resident((D_MODEL, D_FF)), _resident((D_FF, D_MODEL))]


def _ffn(x2, gain, wg, wu, wd):
    n = x2.shape[0]
    row = pl.BlockSpec((TOKEN_TILE, D_MODEL), lambda i: (i, 0))
    return pl.pallas_call(
        _ffn_kernel,
        grid=(n // TOKEN_TILE,),
        in_specs=[row] + _ffn_specs(),
        out_specs=row,
        out_shape=jax.ShapeDtypeStruct((n, D_MODEL), F32),
        scratch_shapes=[pltpu.VMEM((TOKEN_TILE, D_FF), BF16)],
        compiler_params=_params(1),
        name="ffn",
    )(x2, gain, wg, wu, wd)


def _mix_ffn(x2, oa, ob, wo, gain, wg, wu, wd):
    n = x2.shape[0]
    row = pl.BlockSpec((TOKEN_TILE, D_MODEL), lambda i: (i, 0))
    half = pl.BlockSpec((TOKEN_TILE, GROUP_WIDTH), lambda i: (i, 0))
    return pl.pallas_call(
        _mix_ffn_kernel,
        grid=(n // TOKEN_TILE,),
        in_specs=[row, half, half, _resident((D_MIX, D_MODEL))] + _ffn_specs(),
        out_specs=row,
        out_shape=jax.ShapeDtypeStruct((n, D_MODEL), F32),
        scratch_shapes=[pltpu.VMEM((TOKEN_TILE, D_FF), BF16)],
        compiler_params=_params(1),
        name="mix_ffn",
    )(x2, oa, ob, wo, gain, wg, wu, wd)


def _head_sum_matrix():
    r = lax.broadcasted_iota(jnp.int32, (LANES, LANES), 0) // HEAD_DIM
    c = lax.broadcasted_iota(jnp.int32, (LANES, LANES), 1) // HEAD_DIM
    return jnp.where(r == c, 1.0, 0.0).astype(BF16)


def _head_rms(y, gain, hsum):
    ss = _dot((y * y).astype(BF16), hsum)
    return y * lax.rsqrt(ss * (1.0 / HEAD_DIM) + RMS_EPS) * gain


def _rope(y, cos, sin_signed, first_half):
    half = HEAD_DIM // 2
    partner = jnp.where(first_half, pltpu.roll(y, LANES - half, 1), pltpu.roll(y, half, 1))
    return y * cos + partner * sin_signed


def _proj_blocks(xn, w_ref, n_blocks):
    per = PROJ_CHUNK // LANES
    for c in range(0, n_blocks, per):
        width = min(per, n_blocks - c) * LANES
        y = _dot(xn, w_ref[:, c * LANES:c * LANES + width])
        for s in range(width // LANES):
            yield c + s, y[:, s * LANES:(s + 1) * LANES]


def _first_half_mask(rows):
    lane = lax.broadcasted_iota(jnp.int32, (rows, LANES), 1)
    return (lane % HEAD_DIM) < (HEAD_DIM // 2)


def _proj_ab_kernel(x_ref, g_ref, w_ref, gq_ref, gk_ref, cos_ref, sin_ref,
                    qkv_ref, qf_ref, km_ref):
    xn = _rms_rows(x_ref[...], g_ref[...]).astype(BF16)
    hsum = _head_sum_matrix()
    cos, sin = cos_ref[...], sin_ref[...]
    first_half = _first_half_mask(TOKEN_TILE)
    blocks_per_kind = D_MIX // LANES
    for cb, y in _proj_blocks(xn, w_ref, 3 * blocks_per_kind):
        kind, pp = divmod(cb, blocks_per_kind)
        if kind < 2 and pp >= N_PAIRS:
            gain = (gq_ref if kind == 0 else gk_ref)[...]
            y = _rope(_head_rms(y, gain, hsum), cos, sin, first_half)
            sl = slice((pp - N_PAIRS) * LANES, (pp - N_PAIRS + 1) * LANES)
            if kind == 0:
                qf_ref[:, sl] = y
            else:
                for jb in range(TOKEN_TILE // MOBA_BLOCK):
                    rows = y[jb * MOBA_BLOCK:(jb + 1) * MOBA_BLOCK, :]
                    km_ref[jb, :, sl] = jnp.mean(rows, axis=0, keepdims=True)
        if kind == 0:
            y = y * ATTN_SCALE
        qkv_ref[:, cb * LANES:(cb + 1) * LANES] = y.astype(BF16)


def _proj_cd_kernel(x_ref, g_ref, w_ref, gqc_ref, gkc_ref, gqd_ref, gkd_ref, bf_ref,
                    cos_ref, sin_ref, qkv_ref, lf_ref, dil_ref):
    xn = _rms_rows(x_ref[...], g_ref[...]).astype(BF16)
    hsum = _head_sum_matrix()
    cos, sin = cos_ref[...], sin_ref[...]
    first_half = _first_half_mask(TOKEN_TILE)
    blocks_per_kind = D_MIX // LANES
    for cb, y in _proj_blocks(xn, w_ref, 3 * blocks_per_kind + 1):
        kind, pp = divmod(cb, blocks_per_kind)
        if kind == 3:
            t = y + bf_ref[...]
            lf_ref[...] = jnp.minimum(t, 0.0) - jnp.log1p(jnp.exp(-jnp.abs(t)))
            continue
        if kind < 2:
            if pp < N_PAIRS:
                y = _head_rms(y, (gqc_ref if kind == 0 else gkc_ref)[...], hsum)
            else:
                y = _head_rms(y, (gqd_ref if kind == 0 else gkd_ref)[...], hsum)
                y = _rope(y, cos, sin, first_half)
        if kind == 0:
            y = y * ATTN_SCALE
        qkv_ref[:, cb * LANES:(cb + 1) * LANES] = y.astype(BF16)
        if pp >= N_PAIRS:
            col = (kind * N_PAIRS + pp - N_PAIRS) * LANES
            dil_ref[:, col:col + LANES] = y


def _proj_common_specs(seq, w_cols):
    row = pl.BlockSpec((TOKEN_TILE, D_MODEL), lambda i: (i, 0))
    tiles_per_seq = seq // TOKEN_TILE
    table = pl.BlockSpec((TOKEN_TILE, LANES), lambda i: (i % tiles_per_seq, 0))
    lane_vec = pl.BlockSpec((1, LANES), lambda i: (0, 0))
    gain = pl.BlockSpec((1, D_MODEL), lambda i: (0, 0))
    return row, gain, _resident((D_MODEL, w_cols)), lane_vec, table


def _proj_ab(x2, seq, gain, w, gq, gk, cos, sin):
    n = x2.shape[0]
    row, gspec, wspec, lane_vec, table = _proj_common_specs(seq, 3 * D_MIX)
    blocks = TOKEN_TILE // MOBA_BLOCK
    return pl.pallas_call(
        _proj_ab_kernel,
        grid=(n // TOKEN_TILE,),
        in_specs=[row, gspec, wspec, lane_vec, lane_vec, table, table],
        out_specs=[pl.BlockSpec((TOKEN_TILE, 3 * D_MIX), lambda i: (i, 0)),
                   pl.BlockSpec((TOKEN_TILE, GROUP_WIDTH), lambda i: (i, 0)),
                   pl.BlockSpec((blocks, 1, GROUP_WIDTH), lambda i: (i, 0, 0))],
        out_shape=[jax.ShapeDtypeStruct((n, 3 * D_MIX), BF16),
                   jax.ShapeDtypeStruct((n, GROUP_WIDTH), F32),
                   jax.ShapeDtypeStruct((n // MOBA_BLOCK, 1, GROUP_WIDTH), F32)],
        compiler_params=_params(1),
        name="proj_ab",
    )(x2, gain, w, gq, gk, cos, sin)


def _proj_cd(x2, seq, gain, w, gqc, gkc, gqd, gkd, bf, cos, sin):
    n = x2.shape[0]
    row, gspec, wspec, lane_vec, table = _proj_common_specs(seq, 3 * D_MIX + LANES)
    return pl.pallas_call(
        _proj_cd_kernel,
        grid=(n // TOKEN_TILE,),
        in_specs=[row, gspec, wspec, lane_vec, lane_vec, lane_vec, lane_vec, lane_vec, table, table],
        out_specs=[pl.BlockSpec((TOKEN_TILE, 3 * D_MIX), lambda i: (i, 0)),
                   pl.BlockSpec((TOKEN_TILE, LANES), lambda i: (i, 0)),
                   pl.BlockSpec((TOKEN_TILE, 3 * GROUP_WIDTH), lambda i: (i, 0))],
        out_shape=[jax.ShapeDtypeStruct((n, 3 * D_MIX), BF16),
                   jax.ShapeDtypeStruct((n, LANES), F32),
                   jax.ShapeDtypeStruct((n, 3 * GROUP_WIDTH), F32)],
        compiler_params=_params(1),
        name="proj_cd",
    )(x2, gain, w, gqc, gkc, gqd, gkd, bf, cos, sin)


def _cumsum_kernel(lf_ref, c_ref, carry_ref):
    @pl.when(pl.program_id(1) == 0)
    def _():
        carry_ref[...] = jnp.zeros_like(carry_ref)

    t = CUMSUM_TILE
    r = lax.broadcasted_iota(jnp.int32, (t, t), 0)
    c = lax.broadcasted_iota(jnp.int32, (t, t), 1)
    lower = jnp.where(c <= r, 1.0, 0.0).astype(BF16)
    lf = lf_ref[...]
    hi = lf.astype(BF16)
    mid, lo = _split2(lf - hi.astype(F32))
    out = _dot(lower, hi) + _dot(lower, mid) + _dot(lower, lo) + carry_ref[0:1, :]
    c_ref[...] = out
    carry_ref[0:1, :] = out[t - 1:t, :]


def _cumsum(lf3):
    b, s, _ = lf3.shape
    blk = pl.BlockSpec((None, CUMSUM_TILE, LANES), lambda bi, i: (bi, i, 0))
    return pl.pallas_call(
        _cumsum_kernel,
        grid=(b, s // CUMSUM_TILE),
        in_specs=[blk],
        out_specs=blk,
        out_shape=jax.ShapeDtypeStruct(lf3.shape, F32),
        scratch_shapes=[pltpu.VMEM((8, LANES), F32)],
        compiler_params=_params(2),
        name="forget_cumsum",
    )(lf3)


def _pair_queries(q, rows):
    head = lax.broadcasted_iota(jnp.int32, (rows, LANES), 1) // HEAD_DIM
    return head, [jnp.where(head == h, q, jnp.zeros_like(q)) for h in range(HEADS_PER_BLOCK)]


def _softmax_step(s, m, l, acc, vb):
    m_new = jnp.maximum(m, jnp.max(s, axis=1, keepdims=True))
    alpha = jnp.exp(m - m_new)
    p = jnp.exp(s - m_new)
    l = alpha * l + jnp.sum(p, axis=1, keepdims=True)
    acc = alpha * acc + _dot(p.astype(BF16), vb)
    return m_new, l, acc


def _softmax_init(rows):
    return (jnp.full((rows, 1), -jnp.inf, F32), jnp.zeros((rows, 1), F32),
            jnp.zeros((rows, LANES), F32))


def _loop_grouped(n, body, init, groups=(2, 1)):
    done, state = 0, init
    for g in groups:
        def grouped(c, st, g=g, done=done):
            for u in range(g):
                st = body(done + g * c + u, st)
            return st
        count = (n - done) // g
        state = lax.fori_loop(0, count, grouped, state)
        done = done + count * g
    return state


def _causal_bias(t):
    row = lax.broadcasted_iota(jnp.int32, (t, t), 0)
    col = lax.broadcasted_iota(jnp.int32, (t, t), 1)
    return jnp.where(col <= row, 0.0, -MASK_BIG)


def _values_with_ones(vb):
    head = lax.broadcasted_iota(jnp.int32, vb.shape, 1) // HEAD_DIM
    return [jnp.where(head == h, vb, jnp.ones_like(vb)) for h in range(HEADS_PER_BLOCK)]


def _normalize_heads(head, accs):
    outs = [acc / pltpu.roll(acc, HEAD_DIM, 1) for acc in accs]
    return jnp.where(head == 0, outs[0], outs[1])


def _store_heads(o_ref, head, outs):
    o_ref[...] = jnp.where(head == 0, outs[0], outs[1]).astype(o_ref.dtype)


def _sb_kernel(q_ref, sfx_ref, mask_ref, k_ref, v_ref, o_ref):
    t = SB_TILE
    outs = []
    for s in range(ATTN_STEP_BLOCKS):
        i = pl.program_id(2) * ATTN_STEP_BLOCKS + s
        outs.append(_sb_block(i, q_ref[s * t:(s + 1) * t, :], sfx_ref, mask_ref, k_ref, v_ref))
    outs = [finish() for finish in outs]
    o_ref[...] = jnp.concatenate(outs, axis=0).astype(o_ref.dtype)


def _sb_block(i, q, sfx_ref, mask_ref, k_ref, v_ref):
    t = SB_TILE
    head, qhs = _pair_queries(q, t)
    sfx = sfx_ref[...]
    lanes_of = lambda x: jnp.concatenate([x] * (t // LANES), axis=1)

    def block_sums(log_keep):
        hi, lo = _split2(log_keep)
        return _dot(jnp.concatenate([hi, lo], axis=1), sfx)

    def log_gates(qh, kb):
        z = _dot_nt(qh, kb)
        soft = jnp.log(1.0 + jnp.exp(-jnp.abs(z)))
        log_beta = jnp.minimum(z, 0.0) - soft
        return log_beta, log_beta - z

    which = jnp.minimum(i, 1)
    start = pl.multiple_of(jnp.maximum(i - 1, 0) * t, t)
    strict = mask_ref[which, 0]
    strict_neg = mask_ref[which, 1]
    kb = k_ref[pl.ds(start, 2 * t), :]
    vb = v_ref[pl.ds(start, 2 * t), :]
    carry = (i - 2,)
    for qh in qhs:
        log_beta, log_keep = log_gates(qh, kb)
        log_keep = log_keep * strict
        late = block_sums(log_keep[:, t:])
        early = block_sums(log_keep[:, :t])
        between = jnp.concatenate([early[:, :t] + lanes_of(late[:, t:]), late[:, :t]], axis=1)
        w = jnp.exp(log_beta + between + strict_neg)
        carry += (late[:, t:] + early[:, t:], _dot(w.astype(BF16), vb))

    def step(j, run, acc, qh):
        start = pl.multiple_of(j * t, t)
        log_beta, log_keep = log_gates(qh, k_ref[pl.ds(start, t), :])
        sums = block_sums(log_keep)
        w = jnp.exp(log_beta + lanes_of(run) + sums[:, :t])
        return run + sums[:, t:], acc + _dot(w.astype(BF16), v_ref[pl.ds(start, t), :])

    def cond(carry):
        return (carry[0] >= 0) & (jnp.maximum(jnp.max(carry[1]), jnp.max(carry[3])) > -SKIP_LOG)

    def body(carry):
        out = (carry[0] - 1,)
        for h, qh in enumerate(qhs):
            out += step(carry[0], carry[1 + 2 * h], carry[2 + 2 * h], qh)
        return out

    def finish():
        done = lax.while_loop(cond, body, carry)
        return jnp.where(head == 0, done[2], done[4])

    return finish


def _score_bound_row(g_q, g_k):
    bound = 1.05 * HEAD_DIM * ATTN_SCALE * jnp.max(jnp.abs(g_q)) * jnp.max(jnp.abs(g_k))
    return jnp.full((1, LANES), bound, F32)


def _moba_kernel(bound_ref, q_ref, qf_ref, km_ref, onehot_ref, k_ref, v_ref, o_ref):
    t, ch, nq = MOBA_BLOCK, MOBA_CHUNK, MOBA_STEP_BLOCKS
    rows = nq * t
    per = ch // t
    first_own = pl.program_id(2) * nq
    nkb = km_ref.shape[0]
    q, qf = q_ref[...], qf_ref[...]
    head, qhs = _pair_queries(q, rows)
    bound = bound_ref[...][:, 0:1]
    lane = lax.broadcasted_iota(jnp.int32, (rows, LANES), 1)
    km_hi, km_lo = _split2(km_ref[...])
    blk_t = lax.broadcasted_iota(jnp.int32, (nkb, rows), 0)
    blk_tf = blk_t.astype(F32)
    own_t = first_own + lax.broadcasted_iota(jnp.int32, (nkb, rows), 1) // t
    own_row = own_t[0:1, :]
    causal = _causal_bias(t)
    n_chunks = (first_own + nq - 1 + per - 1) // per

    blocked = []
    for h in range(HEADS_PER_BLOCK):
        q_hi, q_lo = _split2(jnp.where(head == h, qf, 0.0))
        gate = _dot_nt(km_hi, q_hi) + _dot_nt(km_lo, q_hi) + _dot_nt(km_hi, q_lo)
        gate = jnp.where(blk_t < own_t, gate, -jnp.inf)
        sel = jnp.zeros((nkb, rows), F32)
        for r in range(MOBA_TOPK):
            best = jnp.max(gate, axis=0, keepdims=True)
            first = jnp.min(jnp.where(gate == best, blk_tf, float(nkb)), axis=0, keepdims=True)
            hit = blk_tf == first
            sel = sel + jnp.where(hit, jnp.where(own_row > r, 1.0, 0.0), 0.0)
            gate = jnp.where(hit, -jnp.inf, gate)
        neg = jnp.concatenate([(sel - 1.0) * MASK_BIG, jnp.zeros((LANES - nkb, rows), F32)], axis=0)
        blocked.append(neg.T)

    def rhs_of(start, count):
        return jnp.concatenate([k_ref[pl.ds(start, count), :], onehot_ref[pl.ds(start, count), :]], axis=1)

    def own_blocks(fn):
        parts = [fn(slice(s * t, (s + 1) * t), pl.multiple_of((first_own + s) * t, t)) for s in range(nq)]
        return tuple(jnp.concatenate(p, axis=0) for p in zip(*parts))

    def shifted():
        lhs, lhs_own = [], []
        for h, qh in enumerate(qhs):
            extra = jnp.where(lane == LANES - 1, -bound, blocked[h])
            lhs.append(jnp.concatenate([qh, extra.astype(BF16)], axis=1))
            extra_own = jnp.where(lane == LANES - 1, -bound, 0.0)
            lhs_own.append(jnp.concatenate([qh, extra_own.astype(BF16)], axis=1))

        def own_pass(rs, start):
            rhs_own = rhs_of(start, t)
            v_owns = _values_with_ones(v_ref[pl.ds(start, t), :])
            return tuple(_dot(jnp.exp(_dot_nt(a[rs], rhs_own) + causal).astype(BF16), vh)
                         for a, vh in zip(lhs_own, v_owns))

        accs = own_blocks(own_pass)

        def sweep(c, accs, row0):
            start = pl.multiple_of(c * ch, ch)
            rhs = rhs_of(start, ch)
            vhs = _values_with_ones(v_ref[pl.ds(start, ch), :])
            new = tuple(acc[row0:] + _dot(jnp.exp(_dot_nt(a[row0:], rhs)).astype(BF16), vh)
                        for a, vh, acc in zip(lhs, vhs, accs))
            if row0 == 0:
                return new
            return tuple(jnp.concatenate([acc[:row0], part], axis=0) for acc, part in zip(accs, new))

        shared = first_own // per
        accs = _loop_grouped(shared, lambda c, accs: sweep(c, accs, 0), accs, groups=(2, 1))
        for extra_chunk in range((nq - 1 + per - 1) // per):
            accs = sweep(shared + extra_chunk, accs, (extra_chunk * per + 1) * t)
        return _normalize_heads(head, accs)

    def running_max():
        lhs = [jnp.concatenate([qh, neg.astype(BF16)], axis=1) for qh, neg in zip(qhs, blocked)]

        def own_pass(rs, start):
            k_own, v_own = k_ref[pl.ds(start, t), :], v_ref[pl.ds(start, t), :]
            out = ()
            for qh in qhs:
                out += _softmax_step(_dot_nt(qh[rs], k_own) + causal, *_softmax_init(t), v_own)
            return out

        flat = own_blocks(own_pass)
        states = tuple(flat[3 * h:3 * h + 3] for h in range(HEADS_PER_BLOCK))

        def body(c, states):
            start = pl.multiple_of(c * ch, ch)
            rhs = rhs_of(start, ch)
            vb = v_ref[pl.ds(start, ch), :]
            return tuple(_softmax_step(_dot_nt(a, rhs), *st, vb) for a, st in zip(lhs, states))

        states = lax.fori_loop(0, n_chunks, body, states)
        outs = [acc / l for _, l, acc in states]
        return jnp.where(head == 0, outs[0], outs[1])

    out = lax.cond(jnp.max(bound) <= SHIFT_LIMIT, shifted, running_max)
    o_ref[...] = out.astype(o_ref.dtype)


def _fox_kernel(bound_ref, q_ref, c_ref, ct_ref, causal_ref, k_ref, v_ref, o_ref):
    t = ATTN_TILE
    qk_bound = bound_ref[...][:, 0:1]
    blocks = []
    for s in range(ATTN_STEP_BLOCKS):
        rs = slice(s * t, (s + 1) * t)
        blocks.append(_fox_block(pl.program_id(2) * ATTN_STEP_BLOCKS + s, q_ref[rs, :], c_ref[rs, :],
                                 qk_bound, ct_ref, causal_ref, k_ref, v_ref))

    def shifted():
        finishers = [start() for start, _ in blocks]
        return jnp.concatenate([finish() for finish in finishers], axis=0)

    def running_max():
        return jnp.concatenate([run() for _, run in blocks], axis=0)

    out = lax.cond(jnp.max(qk_bound) <= SHIFT_LIMIT, shifted, running_max)
    o_ref[...] = out.astype(o_ref.dtype)


def _fox_block(i, q, c_blk, qk_bound, ct_ref, causal_ref, k_ref, v_ref):
    t = ATTN_TILE
    pair = pl.program_id(1)
    head, qhs = _pair_queries(q, t)
    lane = lax.broadcasted_iota(jnp.int32, (t, LANES), 1)
    c_qs = [jnp.sum(jnp.where(lane == HEADS_PER_BLOCK * pair + h, c_blk, 0.0), axis=1, keepdims=True)
            for h in range(HEADS_PER_BLOCK)]
    bounds = [qk_bound] * HEADS_PER_BLOCK

    def block_min_c(h, j):
        start = pl.multiple_of(jnp.maximum(j, 0) * t, t)
        return jnp.min(ct_ref[h:h + 1, pl.ds(start, t)])

    def shifted():
        lhs = [jnp.concatenate([qh, jnp.where(lane == LANES - 1, -bounds[h], 0.0).astype(BF16)], axis=1)
               for h, qh in enumerate(qhs)]
        ones_col = {rows: jnp.where(lax.broadcasted_iota(jnp.int32, (rows, LANES), 1) == LANES - 1,
                                    1.0, 0.0).astype(BF16) for rows in (t, 2 * t)}
        level = [jnp.max(c_qs[h] + 2.0 * bounds[h]) for h in range(HEADS_PER_BLOCK)]

        def attend(start, rows, bias, accs):
            rhs = jnp.concatenate([k_ref[pl.ds(start, rows), :], ones_col[rows]], axis=1)
            vhs = _values_with_ones(v_ref[pl.ds(start, rows), :])
            out = ()
            for h, a in enumerate(lhs):
                s = _dot_nt(a, rhs) + (c_qs[h] - ct_ref[h:h + 1, pl.ds(start, rows)])
                p = jnp.exp(s if bias is None else s + bias)
                out += (accs[h] + _dot(p.astype(BF16), vhs[h]),)
            return out

        zeros = jnp.zeros((t, LANES), F32)
        start = pl.multiple_of(jnp.maximum(i - 1, 0) * t, t)
        carry = (i - 2,) + attend(start, 2 * t, causal_ref[jnp.minimum(i, 1)], (zeros,) * 2)

        def cond(carry):
            j = carry[0]
            margin = [level[h] - block_min_c(h, j) for h in range(HEADS_PER_BLOCK)]
            return (j >= 0) & (jnp.maximum(margin[0], margin[1]) > -SKIP_LOG)

        def body(carry):
            start = pl.multiple_of(carry[0] * t, t)
            return (carry[0] - 1,) + attend(start, t, None, carry[1:])

        return lambda: _normalize_heads(head, lax.while_loop(cond, body, carry)[1:])

    def running_max():
        reach =[bounds[h] + c_qs[h] for h in range(HEADS_PER_BLOCK)]

        def step(j, states, extra):
            start = pl.multiple_of(j * t, t)
            kb = k_ref[pl.ds(start, t), :]
            vb = v_ref[pl.ds(start, t), :]
            out = ()
            for h, qh in enumerate(qhs):
                s = _dot_nt(qh, kb) + (c_qs[h] - ct_ref[h:h + 1, pl.ds(start, t)])
                if extra is not None:
                    s = s + extra
                out += _softmax_step(s, *states[3 * h:3 * h + 3], vb)
            return out

        carry = (i - 1,) + step(i, _softmax_init(t) * HEADS_PER_BLOCK, _causal_bias(t))

        def cond(carry):
            j = carry[0]
            margin = [jnp.max(reach[h] - carry[1 + 3 * h]) - block_min_c(h, j)
                      for h in range(HEADS_PER_BLOCK)]
            return (j >= 0) & (jnp.maximum(margin[0], margin[1]) > -SKIP_LOG)

        def body(carry):
            return (carry[0] - 1,) + step(carry[0], carry[1:], None)

        carry = lax.while_loop(cond, body, carry)
        outs = [carry[3 + 3 * h] / carry[2 + 3 * h] for h in range(HEADS_PER_BLOCK)]
        return jnp.where(head == 0, outs[0], outs[1])

    return shifted, running_max


def _dilated_kernel(bound_ref, q_ref, tbl_ref, k_ref, v_ref, o_ref):
    t, ch = ATTN_TILE, DILATED_CHUNK
    i = pl.program_id(2)
    q = q_ref[...]
    head, qhs = _pair_queries(q, t)
    bound = bound_ref[...][:, 0:1]
    half = i // 2
    parity = i - 2 * half
    n_chunks = jnp.minimum(half + 1, tbl_ref.shape[0] // 2)

    def chunk(c):
        start = pl.multiple_of((half - c) * ch, ch)
        return k_ref[pl.ds(start, ch), :], v_ref[pl.ds(start, ch), :], tbl_ref[parity + 2 * c]

    def shifted():
        lane = lax.broadcasted_iota(jnp.int32, (t, LANES), 1)
        lhs = [jnp.concatenate(
            [qh, jnp.where(lane == LANES - 1, -bound, 0.0).astype(BF16)],
            axis=1) for h, qh in enumerate(qhs)]
        ones_col = jnp.where(lax.broadcasted_iota(jnp.int32, (ch, LANES), 1) == LANES - 1, 1.0, 0.0).astype(BF16)

        def body(c, states):
            kb, vb, bias = chunk(c)
            rhs = jnp.concatenate([kb, ones_col], axis=1)
            out = ()
            for a, (l, acc) in zip(lhs, states):
                p = jnp.exp(_dot_nt(a, rhs) + bias)
                for c0 in range(0, ch, LANES):
                    l = l + p[:, c0:c0 + LANES]
                out += ((l, acc + _dot(p.astype(BF16), vb)),)
            return out

        zeros = jnp.zeros((t, LANES), F32)
        states = _loop_grouped(n_chunks, body, ((zeros, zeros),) * HEADS_PER_BLOCK)
        outs = [acc / jnp.sum(l, axis=1, keepdims=True) for l, acc in states]
        return jnp.where(head == 0, outs[0], outs[1])

    def running_max():
        def body(c, states):
            kb, vb, bias = chunk(c)
            return tuple(_softmax_step(_dot_nt(qh, kb) + bias, *st, vb) for qh, st in zip(qhs, states))

        states = lax.fori_loop(0, n_chunks, body, (_softmax_init(t),) * HEADS_PER_BLOCK)
        outs = [acc / l for _, l, acc in states]
        return jnp.where(head == 0, outs[0], outs[1])

    out = lax.cond(jnp.max(bound) <= SHIFT_LIMIT, shifted, running_max)
    o_ref[...] = out.astype(o_ref.dtype)


def _dilated_class_kernel(shift_ref, band_ref, q_ref, k_ref, v_ref, o_ref, tot_ref):
    ct = CLASS_TILE
    slab = pl.program_id(2)
    head = lax.broadcasted_iota(jnp.int32, (ct, LANES), 1) // HEAD_DIM
    shift = shift_ref[...][:, 0:1]
    band_first = band_ref[0] - shift
    band = band_ref[1] - shift
    band_edge = jnp.where(slab == 0, band_first, band)

    def rows(ref, start, count, d):
        return ref[pl.ds(start, count), :] if d == 1 else ref[pl.ds(start, count, stride=d), :]

    for branch, (window, d) in enumerate(DILATED_BRANCHES):
        assert window == ct * d and DILATED_SLAB % (ct * d) == 0
        for r in range(d):
            for u in range(DILATED_SLAB // (ct * d)):
                row0 = r + d * ct * u
                qb = rows(q_ref, row0, ct, d).astype(BF16)
                key0 = slab * DILATED_SLAB + (row0 - d * ct)
                start = jnp.maximum(key0, r) if u == 0 else key0
                kb = rows(k_ref, start, 2 * ct, d).astype(BF16)
                vhs = _values_with_ones(rows(v_ref, start, 2 * ct, d).astype(BF16))
                for h in range(HEADS_PER_BLOCK):
                    qh = jnp.where(head == h, qb, jnp.zeros_like(qb))
                    p = jnp.exp(_dot_nt(qh, kb) + (band_edge if u == 0 else band))
                    acc = _dot(p.astype(BF16), vhs[h])
                    if branch == 0:
                        tot_ref[h, pl.ds(row0, ct), :] = acc
                    elif d == 1:
                        tot_ref[h, pl.ds(row0, ct), :] += acc
                    else:
                        tot_ref[h, pl.ds(row0, ct, stride=d), :] += acc

    slab_head = lax.broadcasted_iota(jnp.int32, (DILATED_SLAB, LANES), 1) // HEAD_DIM
    out = _normalize_heads(slab_head, [tot_ref[h] for h in range(HEADS_PER_BLOCK)])
    o_ref[...] = out.astype(o_ref.dtype)


def _dilated_classes(dil3, shift_row):
    b, seq, _ = dil3.shape
    ct = CLASS_TILE
    r = np.arange(ct)[:, None]
    c = np.arange(2 * ct)[None, :]
    band = np.stack([c <= r, (c >= r) & (c <= r + ct)]).astype(np.float32)
    band = jnp.asarray((band - 1.0) * MASK_BIG)
    return pl.pallas_call(
        _dilated_class_kernel,
        grid=(b, N_PAIRS, seq // DILATED_SLAB),
        in_specs=[pl.BlockSpec((1, LANES), lambda b, p, i: (0, 0)),
                  _resident(band.shape),
                  pl.BlockSpec((None, DILATED_SLAB, LANES), lambda b, p, i: (b, i, p)),
                  pl.BlockSpec((None, seq, LANES), lambda b, p, i: (b, 0, N_PAIRS + p)),
                  pl.BlockSpec((None, seq, LANES), lambda b, p, i: (b, 0, 2 * N_PAIRS + p))],
        out_specs=pl.BlockSpec((None, DILATED_SLAB, LANES), lambda b, p, i: (b, i, p)),
        out_shape=jax.ShapeDtypeStruct((b, seq, GROUP_WIDTH), BF16),
        scratch_shapes=[pltpu.VMEM((HEADS_PER_BLOCK, DILATED_SLAB, LANES), F32)],
        compiler_params=_params(3),
        name="dilated_classes",
    )(shift_row, band, dil3, dil3, dil3)


def _qkv_specs(seq, tq, group):
    blocks_per_kind = D_MIX // LANES
    off = group * N_PAIRS
    q = pl.BlockSpec((None, tq, LANES), lambda b, p, i: (b, i, off + p))
    k = pl.BlockSpec((None, seq, LANES), lambda b, p, i: (b, 0, blocks_per_kind + off + p))
    v = pl.BlockSpec((None, seq, LANES), lambda b, p, i: (b, 0, 2 * blocks_per_kind + off + p))
    return q, k, v


def _attn_call(kernel_fn, name, qkv3, tq, group, extra_specs, extra_args, bound_row=None):
    b, seq, _ = qkv3.shape
    q, k, v = _qkv_specs(seq, tq, group)
    lead_specs = [] if bound_row is None else [pl.BlockSpec((1, LANES), lambda b, p, i: (0, 0))]
    lead_args = [] if bound_row is None else [bound_row]
    return pl.pallas_call(
        kernel_fn,
        grid=(b, N_PAIRS, seq // tq),
        in_specs=lead_specs + [q] + list(extra_specs) + [k, v],
        out_specs=pl.BlockSpec((None, tq, LANES), lambda b, p, i: (b, i, p)),
        out_shape=jax.ShapeDtypeStruct((b, seq, GROUP_WIDTH), BF16),
        compiler_params=_params(3),
        name=name,
    )(*lead_args, qkv3, *extra_args, qkv3, qkv3)


def _dilated_bias_table(t, ch):
    reach = max(w for w, _ in DILATED_BRANCHES)
    n = 2 * ((reach // t + 1 + ch // t) // 2)
    r = np.arange(t)[:, None]
    c = np.arange(ch)[None, :]
    out = np.zeros((n, t, ch), np.float32)
    for e in range(n):
        dist = e * t + r - c
        count = np.zeros((t, ch), np.float32)
        for window, dil in DILATED_BRANCHES:
            count += (dist >= 0) & (dist <= window) & (dist % dil == 0)
        out[e] = np.where(count > 0, np.log(np.maximum(count, 1.0)), -MASK_BIG)
    return out


def _suffix_matrix(t):
    j = np.arange(2 * t)[:, None] % t
    s = np.arange(t + LANES)[None, :]
    return jnp.asarray(((s >= t) | (j > s)).astype(np.float32), dtype=BF16)


def _first_step_masks(t, strict):
    r = np.arange(t)[:, None]
    c = np.arange(2 * t)[None, :]
    return np.stack([(c < r + d) if strict else (c <= r + d) for d in (0, t)]).astype(np.float32)


def _block_onehot(seq):
    assert seq // MOBA_BLOCK < LANES
    out = np.zeros((seq, LANES), np.float32)
    out[np.arange(seq), np.arange(seq) // MOBA_BLOCK] = 1.0
    out[:, LANES - 1] = 1.0
    return jnp.asarray(out, dtype=BF16)


def _rope_tables(seq):
    inv = 1.0 / (ROPE_THETA ** (jnp.arange(0, HEAD_DIM, 2, dtype=F32) / HEAD_DIM))
    ang = jnp.arange(seq, dtype=F32)[:, None] * inv[None, :]
    cos, sin = jnp.cos(ang), jnp.sin(ang)
    reps = LANES // HEAD_DIM
    cos_l = jnp.tile(jnp.concatenate([cos, cos], axis=1), (1, reps))
    sin_l = jnp.tile(jnp.concatenate([-sin, sin], axis=1), (1, reps))
    return cos_l, sin_l


def _lane_gain(g):
    return jnp.tile(g.astype(F32), LANES // HEAD_DIM)[None, :]


def kernel(x, norm_ffn1, ffn1_w_gate, ffn1_w_up, ffn1_w_down, norm_mix, w_in_ab, g_q_b, g_k_b,
           w_in_cd, b_f, g_q_c, g_k_c, g_q_d, g_k_d, w_out, norm_ffn2, ffn2_w_gate, ffn2_w_up,
           ffn2_w_down):
    b, seq, d = x.shape
    n = b * seq
    assert d == D_MODEL and n % TOKEN_TILE == 0 and seq % TOKEN_TILE == 0
    assert seq % MOBA_CHUNK == 0 and seq % SB_TILE == 0 and seq % DILATED_CHUNK == 0
    cos, sin = _rope_tables(seq)
    bf = lambda w: w.astype(BF16)
    row = lambda g: g.astype(F32)[None, :]
    x2 = x.reshape(n, d)

    x2 = _ffn(x2, row(norm_ffn1[0]), bf(ffn1_w_gate[0]), bf(ffn1_w_up[0]), bf(ffn1_w_down[0]))
    qkv, qf, km = _proj_ab(x2, seq, row(norm_mix[0]), bf(w_in_ab[0]),
                           _lane_gain(g_q_b[0]), _lane_gain(g_k_b[0]), cos, sin)
    qkv3 = qkv.reshape(b, seq, 3 * D_MIX)
    nkb = seq // MOBA_BLOCK
    sfx =_suffix_matrix(SB_TILE)
    strict = _first_step_masks(SB_TILE, True)
    sb_masks = jnp.asarray(np.stack([strict, (strict - 1.0) * MASK_BIG], axis=1))
    out_a = _attn_call(_sb_kernel, "stick_breaking", qkv3, ATTN_STEP_BLOCKS * SB_TILE, 0,
                       [_resident(sfx.shape), _resident(sb_masks.shape)], [sfx, sb_masks])
    out_b = _attn_call(
        _moba_kernel, "moba", qkv3, MOBA_STEP_BLOCKS * MOBA_BLOCK, 1,
        [pl.BlockSpec((None, MOBA_STEP_BLOCKS * MOBA_BLOCK, LANES), lambda b, p, i: (b, i, p)),
         pl.BlockSpec((None, nkb, LANES), lambda b, p, i: (b, 0, p)),
         _resident((seq, LANES))],
        [qf.reshape(b, seq, GROUP_WIDTH), km.reshape(b, nkb, GROUP_WIDTH), _block_onehot(seq)],
        bound_row=_score_bound_row(g_q_b[0], g_k_b[0]))
    x2 = _mix_ffn(x2, out_a.reshape(n, GROUP_WIDTH), out_b.reshape(n, GROUP_WIDTH), bf(w_out[0]),
                  row(norm_ffn2[0]), bf(ffn2_w_gate[0]), bf(ffn2_w_up[0]), bf(ffn2_w_down[0]))

    x2 = _ffn(x2, row(norm_ffn1[1]), bf(ffn1_w_gate[1]), bf(ffn1_w_up[1]), bf(ffn1_w_down[1]))
    w_cd = jnp.pad(bf(w_in_cd[0]), ((0, 0), (0, LANES - GROUP_HEADS)))
    b_f_l = jnp.pad(b_f[0].astype(F32), (0, LANES - GROUP_HEADS))[None, :]
    qkv, lf, dil = _proj_cd(
        x2, seq, row(norm_mix[1]), w_cd, _lane_gain(g_q_c[0]), _lane_gain(g_k_c[0]),
        _lane_gain(g_q_d[0]), _lane_gain(g_k_d[0]), b_f_l, cos, sin)
    qkv3 = qkv.reshape(b, seq, 3 * D_MIX)
    c = _cumsum(lf.reshape(b, seq, LANES))
    c_t = jnp.swapaxes(c[:, :, :GROUP_HEADS], 1, 2).reshape(b, N_PAIRS, HEADS_PER_BLOCK, seq)
    causal = jnp.asarray((_first_step_masks(ATTN_TILE, False) - 1.0) * MASK_BIG)
    out_c = _attn_call(
        _fox_kernel, "forgetting", qkv3, ATTN_STEP_BLOCKS * ATTN_TILE, 0,
        [pl.BlockSpec((None, ATTN_STEP_BLOCKS * ATTN_TILE, LANES), lambda b, p, i: (b, i, 0)),
         pl.BlockSpec((None, None, HEADS_PER_BLOCK, seq), lambda b, p, i: (b, p, 0, 0)),
         _resident(causal.shape)],
        [c, c_t, causal], bound_row=_score_bound_row(g_q_c[0], g_k_c[0]))
    table = jnp.asarray(_dilated_bias_table(ATTN_TILE, DILATED_CHUNK))
    dil_bound = _score_bound_row(g_q_d[0], g_k_d[0])

    def dilated_dense():
        return _attn_call(_dilated_kernel, "dilated", qkv3, ATTN_TILE, 1, [_resident(table.shape)],
                          [table], bound_row=dil_bound)

    if seq % DILATED_SLAB == 0 and seq >= 2 * DILATED_SLAB:
        out_d = lax.cond(dil_bound[0, 0] <= SHIFT_LIMIT,
                         lambda: _dilated_classes(dil.reshape(b, seq, 3 * GROUP_WIDTH), dil_bound),
                         dilated_dense)
    else:
        out_d = dilated_dense()
    x2 = _mix_ffn(x2, out_c.reshape(n, GROUP_WIDTH), out_d.reshape(n, GROUP_WIDTH), bf(w_out[1]),
                  row(norm_ffn2[1]), bf(ffn2_w_gate[1]), bf(ffn2_w_up[1]), bf(ffn2_w_down[1]))
    return x2.reshape(b, seq, d)
```

```python
import numpy as np
import jax
import jax.numpy as jnp
from jax import lax
from jax.experimental import pallas as pl
from jax.experimental.pallas import tpu as pltpu

F32 = jnp.float32
BF16 = jnp.bfloat16

D_MODEL = 1024
HEAD_DIM = 64
D_MIX = 1024
D_FF = 2816
GROUP_HEADS = 8
GROUP_WIDTH = GROUP_HEADS * HEAD_DIM
LANES = 128
HEADS_PER_BLOCK = LANES // HEAD_DIM
N_PAIRS = GROUP_WIDTH // LANES
RMS_EPS = 1e-6
ROPE_THETA = 10000.0
FFN_RES_WEIGHT = 0.5
ATTN_SCALE = HEAD_DIM ** -0.5
MOBA_BLOCK = 256
MOBA_TOPK = 3
DILATED_BRANCHES = ((128, 1), (512, 4), (2048, 16))
SKIP_LOG = 88.0
VMEM_LIMIT_BYTES = 56 * 1024 * 1024

TOKEN_TILE = 512
FF_CHUNK = 256
PROJ_CHUNK = 256
SB_TILE = 256
ATTN_TILE = 256
MOBA_CHUNK = 2 * MOBA_BLOCK
MOBA_STEP_BLOCKS = 4
ATTN_STEP_BLOCKS = 4
DILATED_CHUNK = 2 * ATTN_TILE
CLASS_TILE = 128
DILATED_SLAB = 16 * CLASS_TILE
MASK_BIG = 1e30
SHIFT_LIMIT = 40.0
CUMSUM_TILE = 256

_NT = (((1,), (1,)), ((), ()))


def _params(n_grid):
    return pltpu.CompilerParams(
        dimension_semantics=("arbitrary",) * n_grid,
        vmem_limit_bytes=VMEM_LIMIT_BYTES)


def _resident(shape):
    nd = len(shape)
    return pl.BlockSpec(shape, lambda *_: (0,) * nd, pipeline_mode=pl.Buffered(1))


def _dot(a, b):
    return jnp.dot(a, b, preferred_element_type=F32)


def _dot_nt(a, b):
    return lax.dot_general(a, b, _NT, preferred_element_type=F32)


def _split2(x):
    hi = x.astype(BF16)
    lo = (x - hi.astype(F32)).astype(BF16)
    return hi, lo


def _rms_rows(x, gain):
    return x * lax.rsqrt(jnp.mean(x * x, axis=-1, keepdims=True) + RMS_EPS) * gain


def _swiglu_residual(x, g_ref, wg_ref, wu_ref, wd_ref, h_ref):
    xn = _rms_rows(x, g_ref[...]).astype(BF16)
    for c in range(D_FF // FF_CHUNK):
        sl = slice(c * FF_CHUNK, (c + 1) * FF_CHUNK)
        gate = _dot(xn, wg_ref[:, sl])
        up = _dot(xn, wu_ref[:, sl])
        h_ref[:, sl] = (gate * jax.nn.sigmoid(gate) * up).astype(BF16)
    return x + FFN_RES_WEIGHT * _dot(h_ref[...], wd_ref[...])


def _ffn_kernel(x_ref, g_ref, wg_ref, wu_ref, wd_ref, o_ref, h_ref):
    o_ref[...] = _swiglu_residual(x_ref[...], g_ref, wg_ref, wu_ref, wd_ref, h_ref)


def _mix_ffn_kernel(x_ref, a_ref, b_ref, wo_ref, g_ref, wg_ref, wu_ref, wd_ref, o_ref, h_ref):
    x = (x_ref[...] + _dot(a_ref[...], wo_ref[0:GROUP_WIDTH, :])
         + _dot(b_ref[...], wo_ref[GROUP_WIDTH:D_MIX, :]))
    o_ref[...] = _swiglu_residual(x, g_ref, wg_ref, wu_ref, wd_ref, h_ref)


def _ffn_specs():
    return [pl.BlockSpec((1, D_MODEL), lambda i: (0, 0)),
            _resident((D_MODEL, D_FF)), _resident((D_MODEL, D_FF)), _resident((D_FF, D_MODEL))]


def _ffn(x2, gain, wg, wu, wd):
    n = x2.shape[0]
    row = pl.BlockSpec((TOKEN_TILE, D_MODEL), lambda i: (i, 0))
    return pl.pallas_call(
        _ffn_kernel,
        grid=(n // TOKEN_TILE,),
        in_specs=[row] + _ffn_specs(),
        out_specs=row,
        out_shape=jax.ShapeDtypeStruct((n, D_MODEL), F32),
        scratch_shapes=[pltpu.VMEM((TOKEN_TILE, D_FF), BF16)],
        compiler_params=_params(1),
        name="ffn",
    )(x2, gain, wg, wu, wd)


def _mix_ffn(x2, oa, ob, wo, gain, wg, wu, wd):
    n = x2.shape[0]
    row = pl.BlockSpec((TOKEN_TILE, D_MODEL), lambda i: (i, 0))
    half = pl.BlockSpec((TOKEN_TILE, GROUP_WIDTH), lambda i: (i, 0))
    return pl.pallas_call(
        _mix_ffn_kernel,
        grid=(n // TOKEN_TILE,),
        in_specs=[row, half, half, _resident((D_MIX, D_MODEL))] + _ffn_specs(),
        out_specs=row,
        out_shape=jax.ShapeDtypeStruct((n, D_MODEL), F32),
        scratch_shapes=[pltpu.VMEM((TOKEN_TILE, D_FF), BF16)],
        compiler_params=_params(1),
        name="mix_ffn",
    )(x2, oa, ob, wo, gain, wg, wu, wd)


def _head_sum_matrix():
    r = lax.broadcasted_iota(jnp.int32, (LANES, LANES), 0) // HEAD_DIM
    c = lax.broadcasted_iota(jnp.int32, (LANES, LANES), 1) // HEAD_DIM
    return jnp.where(r == c, 1.0, 0.0).astype(BF16)


def _head_rms(y, gain, hsum):
    ss = _dot((y * y).astype(BF16), hsum)
    return y * lax.rsqrt(ss * (1.0 / HEAD_DIM) + RMS_EPS) * gain


def _rope(y, cos, sin_signed, first_half):
    half = HEAD_DIM // 2
    partner = jnp.where(first_half, pltpu.roll(y, LANES - half, 1), pltpu.roll(y, half, 1))
    return y * cos + partner * sin_signed


def _proj_blocks(xn, w_ref, n_blocks):
    per = PROJ_CHUNK // LANES
    for c in range(0, n_blocks, per):
        width = min(per, n_blocks - c) * LANES
        y = _dot(xn, w_ref[:, c * LANES:c * LANES + width])
        for s in range(width // LANES):
            yield c + s, y[:, s * LANES:(s + 1) * LANES]


def _first_half_mask(rows):
    lane = lax.broadcasted_iota(jnp.int32, (rows, LANES), 1)
    return (lane % HEAD_DIM) < (HEAD_DIM // 2)


def _proj_ab_kernel(x_ref, g_ref, w_ref, gq_ref, gk_ref, cos_ref, sin_ref,
                    qkv_ref, qf_ref, km_ref):
    xn = _rms_rows(x_ref[...], g_ref[...]).astype(BF16)
    hsum = _head_sum_matrix()
    cos, sin = cos_ref[...], sin_ref[...]
    first_half = _first_half_mask(TOKEN_TILE)
    blocks_per_kind = D_MIX // LANES
    for cb, y in _proj_blocks(xn, w_ref, 3 * blocks_per_kind):
        kind, pp = divmod(cb, blocks_per_kind)
        if kind < 2 and pp >= N_PAIRS:
            gain = (gq_ref if kind == 0 else gk_ref)[...]
            y = _rope(_head_rms(y, gain, hsum), cos, sin, first_half)
            sl = slice((pp - N_PAIRS) * LANES, (pp - N_PAIRS + 1) * LANES)
            if kind == 0:
                qf_ref[:, sl] = y
            else:
                for jb in range(TOKEN_TILE // MOBA_BLOCK):
                    rows = y[jb * MOBA_BLOCK:(jb + 1) * MOBA_BLOCK, :]
                    km_ref[jb, :, sl] = jnp.mean(rows, axis=0, keepdims=True)
        if kind == 0:
            y = y * ATTN_SCALE
        qkv_ref[:, cb * LANES:(cb + 1) * LANES] = y.astype(BF16)


def _proj_cd_kernel(x_ref, g_ref, w_ref, gqc_ref, gkc_ref, gqd_ref, gkd_ref, bf_ref,
                    cos_ref, sin_ref, qkv_ref, lf_ref, dil_ref):
    xn = _rms_rows(x_ref[...], g_ref[...]).astype(BF16)
    hsum = _head_sum_matrix()
    cos, sin = cos_ref[...], sin_ref[...]
    first_half = _first_half_mask(TOKEN_TILE)
    blocks_per_kind = D_MIX // LANES
    for cb, y in _proj_blocks(xn, w_ref, 3 * blocks_per_kind + 1):
        kind, pp = divmod(cb, blocks_per_kind)
        if kind == 3:
            t = y + bf_ref[...]
            lf_ref[...] = jnp.minimum(t, 0.0) - jnp.log1p(jnp.exp(-jnp.abs(t)))
            continue
        if kind < 2:
            if pp < N_PAIRS:
                y = _head_rms(y, (gqc_ref if kind == 0 else gkc_ref)[...], hsum)
            else:
                y = _head_rms(y, (gqd_ref if kind == 0 else gkd_ref)[...], hsum)
                y = _rope(y, cos, sin, first_half)
        if kind == 0:
            y = y * ATTN_SCALE
        qkv_ref[:, cb * LANES:(cb + 1) * LANES] = y.astype(BF16)
        if pp >= N_PAIRS:
            col = (kind * N_PAIRS + pp - N_PAIRS) * LANES
            dil_ref[:, col:col + LANES] = y


def _proj_common_specs(seq, w_cols):
    row = pl.BlockSpec((TOKEN_TILE, D_MODEL), lambda i: (i, 0))
    tiles_per_seq = seq // TOKEN_TILE
    table = pl.BlockSpec((TOKEN_TILE, LANES), lambda i: (i % tiles_per_seq, 0))
    lane_vec = pl.BlockSpec((1, LANES), lambda i: (0, 0))
    gain = pl.BlockSpec((1, D_MODEL), lambda i: (0, 0))
    return row, gain, _resident((D_MODEL, w_cols)), lane_vec, table


def _proj_ab(x2, seq, gain, w, gq, gk, cos, sin):
    n = x2.shape[0]
    row, gspec, wspec, lane_vec, table = _proj_common_specs(seq, 3 * D_MIX)
    blocks = TOKEN_TILE // MOBA_BLOCK
    return pl.pallas_call(
        _proj_ab_kernel,
        grid=(n // TOKEN_TILE,),
        in_specs=[row, gspec, wspec, lane_vec, lane_vec, table, table],
        out_specs=[pl.BlockSpec((TOKEN_TILE, 3 * D_MIX), lambda i: (i, 0)),
                   pl.BlockSpec((TOKEN_TILE, GROUP_WIDTH), lambda i: (i, 0)),
                   pl.BlockSpec((blocks, 1, GROUP_WIDTH), lambda i: (i, 0, 0))],
        out_shape=[jax.ShapeDtypeStruct((n, 3 * D_MIX), BF16),
                   jax.ShapeDtypeStruct((n, GROUP_WIDTH), F32),
                   jax.ShapeDtypeStruct((n // MOBA_BLOCK, 1, GROUP_WIDTH), F32)],
        compiler_params=_params(1),
        name="proj_ab",
    )(x2, gain, w, gq, gk, cos, sin)


def _proj_cd(x2, seq, gain, w, gqc, gkc, gqd, gkd, bf, cos, sin):
    n = x2.shape[0]
    row, gspec, wspec, lane_vec, table = _proj_common_specs(seq, 3 * D_MIX + LANES)
    return pl.pallas_call(
        _proj_cd_kernel,
        grid=(n // TOKEN_TILE,),
        in_specs=[row, gspec, wspec, lane_vec, lane_vec, lane_vec, lane_vec, lane_vec, table, table],
        out_specs=[pl.BlockSpec((TOKEN_TILE, 3 * D_MIX), lambda i: (i, 0)),
                   pl.BlockSpec((TOKEN_TILE, LANES), lambda i: (i, 0)),
                   pl.BlockSpec((TOKEN_TILE, 3 * GROUP_WIDTH), lambda i: (i, 0))],
        out_shape=[jax.ShapeDtypeStruct((n, 3 * D_MIX), BF16),
                   jax.ShapeDtypeStruct((n, LANES), F32),
                   jax.ShapeDtypeStruct((n, 3 * GROUP_WIDTH), F32)],
        compiler_params=_params(1),
        name="proj_cd",
    )(x2, gain, w, gqc, gkc, gqd, gkd, bf, cos, sin)


def _cumsum_kernel(lf_ref, c_ref, carry_ref):
    @pl.when(pl.program_id(1) == 0)
    def _():
        carry_ref[...] = jnp.zeros_like(carry_ref)

    t = CUMSUM_TILE
    r = lax.broadcasted_iota(jnp.int32, (t, t), 0)
    c = lax.broadcasted_iota(jnp.int32, (t, t), 1)
    lower = jnp.where(c <= r, 1.0, 0.0).astype(BF16)
    lf = lf_ref[...]
    hi = lf.astype(BF16)
    mid, lo = _split2(lf - hi.astype(F32))
    out = _dot(lower, hi) + _dot(lower, mid) + _dot(lower, lo) + carry_ref[0:1, :]
    c_ref[...] = out
    carry_ref[0:1, :] = out[t - 1:t, :]


def _cumsum(lf3):
    b, s, _ = lf3.shape
    blk = pl.BlockSpec((None, CUMSUM_TILE, LANES), lambda bi, i: (bi, i, 0))
    return pl.pallas_call(
        _cumsum_kernel,
        grid=(b, s // CUMSUM_TILE),
        in_specs=[blk],
        out_specs=blk,
        out_shape=jax.ShapeDtypeStruct(lf3.shape, F32),
        scratch_shapes=[pltpu.VMEM((8, LANES), F32)],
        compiler_params=_params(2),
        name="forget_cumsum",
    )(lf3)


def _pair_queries(q, rows):
    head = lax.broadcasted_iota(jnp.int32, (rows, LANES), 1) // HEAD_DIM
    return head, [jnp.where(head == h, q, jnp.zeros_like(q)) for h in range(HEADS_PER_BLOCK)]


def _softmax_step(s, m, l, acc, vb):
    m_new = jnp.maximum(m, jnp.max(s, axis=1, keepdims=True))
    alpha = jnp.exp(m - m_new)
    p = jnp.exp(s - m_new)
    l = alpha * l + jnp.sum(p, axis=1, keepdims=True)
    acc = alpha * acc + _dot(p.astype(BF16), vb)
    return m_new, l, acc


def _softmax_init(rows):
    return (jnp.full((rows, 1), -jnp.inf, F32), jnp.zeros((rows, 1), F32),
            jnp.zeros((rows, LANES), F32))


def _loop_grouped(n, body, init, groups=(2, 1)):
    done, state = 0, init
    for g in groups:
        def grouped(c, st, g=g, done=done):
            for u in range(g):
                st = body(done + g * c + u, st)
            return st
        count = (n - done) // g
        state = lax.fori_loop(0, count, grouped, state)
        done = done + count * g
    return state


def _causal_bias(t):
    row = lax.broadcasted_iota(jnp.int32, (t, t), 0)
    col = lax.broadcasted_iota(jnp.int32, (t, t), 1)
    return jnp.where(col <= row, 0.0, -MASK_BIG)


def _values_with_ones(vb):
    head = lax.broadcasted_iota(jnp.int32, vb.shape, 1) // HEAD_DIM
    return [jnp.where(head == h, vb, jnp.ones_like(vb)) for h in range(HEADS_PER_BLOCK)]


def _normalize_heads(head, accs):
    outs = [acc / pltpu.roll(acc, HEAD_DIM, 1) for acc in accs]
    return jnp.where(head == 0, outs[0], outs[1])


def _sb_kernel(q_ref, sfx_ref, mask_ref, k_ref, v_ref, o_ref):
    t = SB_TILE
    outs = []
    for s in range(ATTN_STEP_BLOCKS):
        i = pl.program_id(2) * ATTN_STEP_BLOCKS + s
        outs.append(_sb_block(i, q_ref[s * t:(s + 1) * t, :], sfx_ref, mask_ref, k_ref, v_ref))
    outs = [finish() for finish in outs]
    o_ref[...] = jnp.concatenate(outs, axis=0).astype(o_ref.dtype)


def _sb_block(i, q, sfx_ref, mask_ref, k_ref, v_ref):
    t = SB_TILE
    head, qhs = _pair_queries(q, t)
    sfx = sfx_ref[...]
    lanes_of = lambda x: jnp.concatenate([x] * (t // LANES), axis=1)

    def block_sums(log_keep):
        hi, lo = _split2(log_keep)
        return _dot(jnp.concatenate([hi, lo], axis=1), sfx)

    def log_gates(qh, kb):
        z = _dot_nt(qh, kb)
        soft = jnp.log(1.0 + jnp.exp(-jnp.abs(z)))
        log_beta = jnp.minimum(z, 0.0) - soft
        return log_beta, log_beta - z

    which = jnp.minimum(i, 1)
    start = pl.multiple_of(jnp.maximum(i - 1, 0) * t, t)
    strict = mask_ref[which, 0]
    strict_neg = mask_ref[which, 1]
    kb = k_ref[pl.ds(start, 2 * t), :]
    vb = v_ref[pl.ds(start, 2 * t), :]
    carry = (i - 2,)
    for qh in qhs:
        log_beta, log_keep = log_gates(qh, kb)
        log_keep = log_keep * strict
        late = block_sums(log_keep[:, t:])
        early = block_sums(log_keep[:, :t])
        between = jnp.concatenate([early[:, :t] + lanes_of(late[:, t:]), late[:, :t]], axis=1)
        w = jnp.exp(log_beta + between + strict_neg)
        carry += (late[:, t:] + early[:, t:], _dot(w.astype(BF16), vb))

    def step(j, run, acc, qh):
        start = pl.multiple_of(j * t, t)
        log_beta, log_keep = log_gates(qh, k_ref[pl.ds(start, t), :])
        sums = block_sums(log_keep)
        w = jnp.exp(log_beta + lanes_of(run) + sums[:, :t])
        return run + sums[:, t:], acc + _dot(w.astype(BF16), v_ref[pl.ds(start, t), :])

    def cond(carry):
        return (carry[0] >= 0) & (jnp.maximum(jnp.max(carry[1]), jnp.max(carry[3])) > -SKIP_LOG)

    def body(carry):
        out = (carry[0] - 1,)
        for h, qh in enumerate(qhs):
            out += step(carry[0], carry[1 + 2 * h], carry[2 + 2 * h], qh)
        return out

    def finish():
        done = lax.while_loop(cond, body, carry)
        return jnp.where(head == 0, done[2], done[4])

    return finish


def _score_bound_row(g_q, g_k):
    bound = 1.05 * HEAD_DIM * ATTN_SCALE * jnp.max(jnp.abs(g_q)) * jnp.max(jnp.abs(g_k))
    return jnp.full((1, LANES), bound, F32)


def _moba_kernel(bound_ref, q_ref, qf_ref, km_ref, onehot_ref, k_ref, v_ref, o_ref):
    t, ch, nq = MOBA_BLOCK, MOBA_CHUNK, MOBA_STEP_BLOCKS
    rows = nq * t
    per = ch // t
    first_own = pl.program_id(2) * nq
    nkb = km_ref.shape[0]
    q, qf = q_ref[...], qf_ref[...]
    head, qhs = _pair_queries(q, rows)
    bound = bound_ref[...][:, 0:1]
    lane = lax.broadcasted_iota(jnp.int32, (rows, LANES), 1)
    km_hi, km_lo = _split2(km_ref[...])
    blk_t = lax.broadcasted_iota(jnp.int32, (nkb, rows), 0)
    blk_tf = blk_t.astype(F32)
    own_t = first_own + lax.broadcasted_iota(jnp.int32, (nkb, rows), 1) // t
    own_row = own_t[0:1, :]
    causal = _causal_bias(t)
    n_chunks = (first_own + nq - 1 + per - 1) // per

    blocked = []
    for h in range(HEADS_PER_BLOCK):
        q_hi, q_lo = _split2(jnp.where(head == h, qf, 0.0))
        gate = _dot_nt(km_hi, q_hi) + _dot_nt(km_lo, q_hi) + _dot_nt(km_hi, q_lo)
        gate = jnp.where(blk_t < own_t, gate, -jnp.inf)
        sel = jnp.zeros((nkb, rows), F32)
        for r in range(MOBA_TOPK):
            best = jnp.max(gate, axis=0, keepdims=True)
            first = jnp.min(jnp.where(gate == best, blk_tf, float(nkb)), axis=0, keepdims=True)
            hit = blk_tf == first
            sel = sel + jnp.where(hit, jnp.where(own_row > r, 1.0, 0.0), 0.0)
            gate = jnp.where(hit, -jnp.inf, gate)
        neg = jnp.concatenate([(sel - 1.0) * MASK_BIG, jnp.zeros((LANES - nkb, rows), F32)], axis=0)
        blocked.append(neg.T)

    def rhs_of(start, count):
        return jnp.concatenate([k_ref[pl.ds(start, count), :], onehot_ref[pl.ds(start, count), :]], axis=1)

    def own_blocks(fn):
        parts = [fn(slice(s * t, (s + 1) * t), pl.multiple_of((first_own + s) * t, t)) for s in range(nq)]
        return tuple(jnp.concatenate(p, axis=0) for p in zip(*parts))

    def shifted():
        lhs, lhs_own = [], []
        for h, qh in enumerate(qhs):
            extra = jnp.where(lane == LANES - 1, -bound, blocked[h])
            lhs.append(jnp.concatenate([qh, extra.astype(BF16)], axis=1))
            extra_own = jnp.where(lane == LANES - 1, -bound, 0.0)
            lhs_own.append(jnp.concatenate([qh, extra_own.astype(BF16)], axis=1))

        def own_pass(rs, start):
            rhs_own = rhs_of(start, t)
            v_owns = _values_with_ones(v_ref[pl.ds(start, t), :])
            return tuple(_dot(jnp.exp(_dot_nt(a[rs], rhs_own) + causal).astype(BF16), vh)
                         for a, vh in zip(lhs_own, v_owns))

        accs = own_blocks(own_pass)

        def sweep(c, accs, row0):
            start = pl.multiple_of(c * ch, ch)
            rhs = rhs_of(start, ch)
            vhs = _values_with_ones(v_ref[pl.ds(start, ch), :])
            new = tuple(acc[row0:] + _dot(jnp.exp(_dot_nt(a[row0:], rhs)).astype(BF16), vh)
                        for a, vh, acc in zip(lhs, vhs, accs))
            if row0 == 0:
                return new
            return tuple(jnp.concatenate([acc[:row0], part], axis=0) for acc, part in zip(accs, new))

        shared = first_own // per
        accs = _loop_grouped(shared, lambda c, accs: sweep(c, accs, 0), accs, groups=(2, 1))
        for extra_chunk in range((nq - 1 + per - 1) // per):
            accs = sweep(shared + extra_chunk, accs, (extra_chunk * per + 1) * t)
        return _normalize_heads(head, accs)

    def running_max():
        lhs = [jnp.concatenate([qh, neg.astype(BF16)], axis=1) for qh, neg in zip(qhs, blocked)]

        def own_pass(rs, start):
            k_own, v_own = k_ref[pl.ds(start, t), :], v_ref[pl.ds(start, t), :]
            out = ()
            for qh in qhs:
                out += _softmax_step(_dot_nt(qh[rs], k_own) + causal, *_softmax_init(t), v_own)
            return out

        flat = own_blocks(own_pass)
        states = tuple(flat[3 * h:3 * h + 3] for h in range(HEADS_PER_BLOCK))

        def body(c, states):
            start = pl.multiple_of(c * ch, ch)
            rhs = rhs_of(start, ch)
            vb = v_ref[pl.ds(start, ch), :]
            return tuple(_softmax_step(_dot_nt(a, rhs), *st, vb) for a, st in zip(lhs, states))

        states = lax.fori_loop(0, n_chunks, body, states)
        outs = [acc / l for _, l, acc in states]
        return jnp.where(head == 0, outs[0], outs[1])

    out = lax.cond(jnp.max(bound) <= SHIFT_LIMIT, shifted, running_max)
    o_ref[...] = out.astype(o_ref.dtype)


def _fox_kernel(bound_ref, q_ref, c_ref, ct_ref, causal_ref, k_ref, v_ref, o_ref):
    t = ATTN_TILE
    qk_bound = bound_ref[...][:, 0:1]
    blocks = []
    for s in range(ATTN_STEP_BLOCKS):
        rs = slice(s * t, (s + 1) * t)
        blocks.append(_fox_block(pl.program_id(2) * ATTN_STEP_BLOCKS + s, q_ref[rs, :], c_ref[rs, :],
                                 qk_bound, ct_ref, causal_ref, k_ref, v_ref))

    def shifted():
        finishers = [start() for start, _ in blocks]
        return jnp.concatenate([finish() for finish in finishers], axis=0)

    def running_max():
        return jnp.concatenate([run() for _, run in blocks], axis=0)

    out = lax.cond(jnp.max(qk_bound) <= SHIFT_LIMIT, shifted, running_max)
    o_ref[...] = out.astype(o_ref.dtype)


def _fox_block(i, q, c_blk, qk_bound, ct_ref, causal_ref, k_ref, v_ref):
    t = ATTN_TILE
    pair = pl.program_id(1)
    head, qhs = _pair_queries(q, t)
    lane = lax.broadcasted_iota(jnp.int32, (t, LANES), 1)
    c_qs = [jnp.sum(jnp.where(lane == HEADS_PER_BLOCK * pair + h, c_blk, 0.0), axis=1, keepdims=True)
            for h in range(HEADS_PER_BLOCK)]
    bounds = [qk_bound] * HEADS_PER_BLOCK

    def block_min_c(h, j):
        start = pl.multiple_of(jnp.maximum(j, 0) * t, t)
        return jnp.min(ct_ref[h:h + 1, pl.ds(start, t)])

    def shifted():
        lhs = [jnp.concatenate([qh, jnp.where(lane == LANES - 1, -bounds[h], 0.0).astype(BF16)], axis=1)
               for h, qh in enumerate(qhs)]
        ones_col = {rows: jnp.where(lax.broadcasted_iota(jnp.int32, (rows, LANES), 1) == LANES - 1,
                                    1.0, 0.0).astype(BF16) for rows in (t, 2 * t)}
        level = [jnp.max(c_qs[h] + 2.0 * bounds[h]) for h in range(HEADS_PER_BLOCK)]

        def attend(start, rows, bias, accs):
            rhs = jnp.concatenate([k_ref[pl.ds(start, rows), :], ones_col[rows]], axis=1)
            vhs = _values_with_ones(v_ref[pl.ds(start, rows), :])
            out = ()
            for h, a in enumerate(lhs):
                s = _dot_nt(a, rhs) + (c_qs[h] - ct_ref[h:h + 1, pl.ds(start, rows)])
                p = jnp.exp(s if bias is None else s + bias)
                out += (accs[h] + _dot(p.astype(BF16), vhs[h]),)
            return out

        zeros = jnp.zeros((t, LANES), F32)
        start = pl.multiple_of(jnp.maximum(i - 1, 0) * t, t)
        carry = (i - 2,) + attend(start, 2 * t, causal_ref[jnp.minimum(i, 1)], (zeros,) * 2)

        def cond(carry):
            j = carry[0]
            margin = [level[h] - block_min_c(h, j) for h in range(HEADS_PER_BLOCK)]
            return (j >= 0) & (jnp.maximum(margin[0], margin[1]) > -SKIP_LOG)

        def body(carry):
            start = pl.multiple_of(carry[0] * t, t)
            return (carry[0] - 1,) + attend(start, t, None, carry[1:])

        return lambda: _normalize_heads(head, lax.while_loop(cond, body, carry)[1:])

    def running_max():
        reach =[bounds[h] + c_qs[h] for h in range(HEADS_PER_BLOCK)]

        def step(j, states, extra):
            start = pl.multiple_of(j * t, t)
            kb = k_ref[pl.ds(start, t), :]
            vb = v_ref[pl.ds(start, t), :]
            out = ()
            for h, qh in enumerate(qhs):
                s = _dot_nt(qh, kb) + (c_qs[h] - ct_ref[h:h + 1, pl.ds(start, t)])
                if extra is not None:
                    s = s + extra
                out += _softmax_step(s, *states[3 * h:3 * h + 3], vb)
            return out

        carry = (i - 1,) + step(i, _softmax_init(t) * HEADS_PER_BLOCK, _causal_bias(t))

        def cond(carry):
            j = carry[0]
            margin = [jnp.max(reach[h] - carry[1 + 3 * h]) - block_min_c(h, j)
                      for h in range(HEADS_PER_BLOCK)]
            return (j >= 0) & (jnp.maximum(margin[0], margin[1]) > -SKIP_LOG)

        def body(carry):
            return (carry[0] - 1,) + step(carry[0], carry[1:], None)

        carry = lax.while_loop(cond, body, carry)
        outs = [carry[3 + 3 * h] / carry[2 + 3 * h] for h in range(HEADS_PER_BLOCK)]
        return jnp.where(head == 0, outs[0], outs[1])

    return shifted, running_max


def _dilated_kernel(bound_ref, q_ref, tbl_ref, k_ref, v_ref, o_ref):
    t, ch = ATTN_TILE, DILATED_CHUNK
    i = pl.program_id(2)
    q = q_ref[...]
    head, qhs = _pair_queries(q, t)
    bound = bound_ref[...][:, 0:1]
    half = i // 2
    parity = i - 2 * half
    n_chunks = jnp.minimum(half + 1, tbl_ref.shape[0] // 2)

    def chunk(c):
        start = pl.multiple_of((half - c) * ch, ch)
        return k_ref[pl.ds(start, ch), :], v_ref[pl.ds(start, ch), :], tbl_ref[parity + 2 * c]

    def shifted():
        lane = lax.broadcasted_iota(jnp.int32, (t, LANES), 1)
        lhs = [jnp.concatenate(
            [qh, jnp.where(lane == LANES - 1, -bound, 0.0).astype(BF16)],
            axis=1) for h, qh in enumerate(qhs)]
        ones_col = jnp.where(lax.broadcasted_iota(jnp.int32, (ch, LANES), 1) == LANES - 1, 1.0, 0.0).astype(BF16)

        def body(c, states):
            kb, vb, bias = chunk(c)
            rhs = jnp.concatenate([kb, ones_col], axis=1)
            out = ()
            for a, (l, acc) in zip(lhs, states):
                p = jnp.exp(_dot_nt(a, rhs) + bias)
                for c0 in range(0, ch, LANES):
                    l = l + p[:, c0:c0 + LANES]
                out += ((l, acc + _dot(p.astype(BF16), vb)),)
            return out

        zeros = jnp.zeros((t, LANES), F32)
        states = _loop_grouped(n_chunks, body, ((zeros, zeros),) * HEADS_PER_BLOCK)
        outs = [acc / jnp.sum(l, axis=1, keepdims=True) for l, acc in states]
        return jnp.where(head == 0, outs[0], outs[1])

    def running_max():
        def body(c, states):
            kb, vb, bias = chunk(c)
            return tuple(_softmax_step(_dot_nt(qh, kb) + bias, *st, vb) for qh, st in zip(qhs, states))

        states = lax.fori_loop(0, n_chunks, body, (_softmax_init(t),) * HEADS_PER_BLOCK)
        outs = [acc / l for _, l, acc in states]
        return jnp.where(head == 0, outs[0], outs[1])

    out = lax.cond(jnp.max(bound) <= SHIFT_LIMIT, shifted, running_max)
    o_ref[...] = out.astype(o_ref.dtype)


def _dilated_class_kernel(shift_ref, band_ref, q_ref, k_ref, v_ref, o_ref, tot_ref):
    ct = CLASS_TILE
    slab = pl.program_id(2)
    head = lax.broadcasted_iota(jnp.int32, (ct, LANES), 1) // HEAD_DIM
    shift = shift_ref[...][:, 0:1]
    band_first = band_ref[0] - shift
    band = band_ref[1] - shift
    band_edge = jnp.where(slab == 0, band_first, band)

    def rows(ref, start, count, d):
        return ref[pl.ds(start, count), :] if d == 1 else ref[pl.ds(start, count, stride=d), :]

    for branch, (window, d) in enumerate(DILATED_BRANCHES):
        assert window == ct * d and DILATED_SLAB % (ct * d) == 0
        for r in range(d):
            for u in range(DILATED_SLAB // (ct * d)):
                row0 = r + d * ct * u
                qb = rows(q_ref, row0, ct, d).astype(BF16)
                key0 = slab * DILATED_SLAB + (row0 - d * ct)
                start = jnp.maximum(key0, r) if u == 0 else key0
                kb = rows(k_ref, start, 2 * ct, d).astype(BF16)
                vhs = _values_with_ones(rows(v_ref, start, 2 * ct, d).astype(BF16))
                for h in range(HEADS_PER_BLOCK):
                    qh = jnp.where(head == h, qb, jnp.zeros_like(qb))
                    p = jnp.exp(_dot_nt(qh, kb) + (band_edge if u == 0 else band))
                    acc = _dot(p.astype(BF16), vhs[h])
                    if branch == 0:
                        tot_ref[h, pl.ds(row0, ct), :] = acc
                    elif d == 1:
                        tot_ref[h, pl.ds(row0, ct), :] += acc
                    else:
                        tot_ref[h, pl.ds(row0, ct, stride=d), :] += acc

    slab_head = lax.broadcasted_iota(jnp.int32, (DILATED_SLAB, LANES), 1) // HEAD_DIM
    out = _normalize_heads(slab_head, [tot_ref[h] for h in range(HEADS_PER_BLOCK)])
    o_ref[...] = out.astype(o_ref.dtype)


def _dilated_classes(dil3, shift_row):
    b, seq, _ = dil3.shape
    ct = CLASS_TILE
    r = np.arange(ct)[:, None]
    c = np.arange(2 * ct)[None, :]
    band = np.stack([c <= r, (c >= r) & (c <= r + ct)]).astype(np.float32)
    band = jnp.asarray((band - 1.0) * MASK_BIG)
    return pl.pallas_call(
        _dilated_class_kernel,
        grid=(b, N_PAIRS, seq // DILATED_SLAB),
        in_specs=[pl.BlockSpec((1, LANES), lambda b, p, i: (0, 0)),
                  _resident(band.shape),
                  pl.BlockSpec((None, DILATED_SLAB, LANES), lambda b, p, i: (b, i, p)),
                  pl.BlockSpec((None, seq, LANES), lambda b, p, i: (b, 0, N_PAIRS + p)),
                  pl.BlockSpec((None, seq, LANES), lambda b, p, i: (b, 0, 2 * N_PAIRS + p))],
        out_specs=pl.BlockSpec((None, DILATED_SLAB, LANES), lambda b, p, i: (b, i, p)),
        out_shape=jax.ShapeDtypeStruct((b, seq, GROUP_WIDTH), BF16),
        scratch_shapes=[pltpu.VMEM((HEADS_PER_BLOCK, DILATED_SLAB, LANES), F32)],
        compiler_params=_params(3),
        name="dilated_classes",
    )(shift_row, band, dil3, dil3, dil3)


def _qkv_specs(seq, tq, group):
    blocks_per_kind = D_MIX // LANES
    off = group * N_PAIRS
    q = pl.BlockSpec((None, tq, LANES), lambda b, p, i: (b, i, off + p))
    k = pl.BlockSpec((None, seq, LANES), lambda b, p, i: (b, 0, blocks_per_kind + off + p))
    v = pl.BlockSpec((None, seq, LANES), lambda b, p, i: (b, 0, 2 * blocks_per_kind + off + p))
    return q, k, v


def _attn_call(kernel_fn, name, qkv3, tq, group, extra_specs, extra_args, bound_row=None):
    b, seq, _ = qkv3.shape
    q, k, v = _qkv_specs(seq, tq, group)
    lead_specs = [] if bound_row is None else [pl.BlockSpec((1, LANES), lambda b, p, i: (0, 0))]
    lead_args = [] if bound_row is None else [bound_row]
    return pl.pallas_call(
        kernel_fn,
        grid=(b, N_PAIRS, seq // tq),
        in_specs=lead_specs + [q] + list(extra_specs) + [k, v],
        out_specs=pl.BlockSpec((None, tq, LANES), lambda b, p, i: (b, i, p)),
        out_shape=jax.ShapeDtypeStruct((b, seq, GROUP_WIDTH), BF16),
        compiler_params=_params(3),
        name=name,
    )(*lead_args, qkv3, *extra_args, qkv3, qkv3)


def _dilated_bias_table(t, ch):
    reach = max(w for w, _ in DILATED_BRANCHES)
    n = 2 * ((reach // t + 1 + ch // t) // 2)
    r = np.arange(t)[:, None]
    c = np.arange(ch)[None, :]
    out = np.zeros((n, t, ch), np.float32)
    for e in range(n):
        dist = e * t + r - c
        count = np.zeros((t, ch), np.float32)
        for window, dil in DILATED_BRANCHES:
            count += (dist >= 0) & (dist <= window) & (dist % dil == 0)
        out[e] = np.where(count > 0, np.log(np.maximum(count, 1.0)), -MASK_BIG)
    return out


def _suffix_matrix(t):
    j = np.arange(2 * t)[:, None] % t
    s = np.arange(t + LANES)[None, :]
    return jnp.asarray(((s >= t) | (j > s)).astype(np.float32), dtype=BF16)


def _first_step_masks(t, strict):
    r = np.arange(t)[:, None]
    c = np.arange(2 * t)[None, :]
    return np.stack([(c < r + d) if strict else (c <= r + d) for d in (0, t)]).astype(np.float32)


def _block_onehot(seq):
    assert seq // MOBA_BLOCK < LANES
    out = np.zeros((seq, LANES), np.float32)
    out[np.arange(seq), np.arange(seq) // MOBA_BLOCK] = 1.0
    out[:, LANES - 1] = 1.0
    return jnp.asarray(out, dtype=BF16)


def _rope_tables(seq):
    inv = 1.0 / (ROPE_THETA ** (jnp.arange(0, HEAD_DIM, 2, dtype=F32) / HEAD_DIM))
    ang = jnp.arange(seq, dtype=F32)[:, None] * inv[None, :]
    cos, sin = jnp.cos(ang), jnp.sin(ang)
    reps = LANES // HEAD_DIM
    cos_l = jnp.tile(jnp.concatenate([cos, cos], axis=1), (1, reps))
    sin_l = jnp.tile(jnp.concatenate([-sin, sin], axis=1), (1, reps))
    return cos_l, sin_l


def _lane_gain(g):
    return jnp.tile(g.astype(F32), LANES // HEAD_DIM)[None, :]


def kernel(x, norm_ffn1, ffn1_w_gate, ffn1_w_up, ffn1_w_down, norm_mix, w_in_ab, g_q_b, g_k_b,
           w_in_cd, b_f, g_q_c, g_k_c, g_q_d, g_k_d, w_out, norm_ffn2, ffn2_w_gate, ffn2_w_up,
           ffn2_w_down):
    b, seq, d = x.shape
    n = b * seq
    assert d == D_MODEL and n % TOKEN_TILE == 0 and seq % TOKEN_TILE == 0
    assert seq % MOBA_CHUNK == 0 and seq % SB_TILE == 0 and seq % DILATED_CHUNK == 0
    cos, sin = _rope_tables(seq)
    bf = lambda w: w.astype(BF16)
    row = lambda g: g.astype(F32)[None, :]
    x2 = x.reshape(n, d)

    x2 = _ffn(x2, row(norm_ffn1[0]), bf(ffn1_w_gate[0]), bf(ffn1_w_up[0]), bf(ffn1_w_down[0]))
    qkv, qf, km = _proj_ab(x2, seq, row(norm_mix[0]), bf(w_in_ab[0]),
                           _lane_gain(g_q_b[0]), _lane_gain(g_k_b[0]), cos, sin)
    qkv3 = qkv.reshape(b, seq, 3 * D_MIX)
    nkb = seq // MOBA_BLOCK
    sfx =_suffix_matrix(SB_TILE)
    strict = _first_step_masks(SB_TILE, True)
    sb_masks = jnp.asarray(np.stack([strict, (strict - 1.0) * MASK_BIG], axis=1))
    out_a = _attn_call(_sb_kernel, "stick_breaking", qkv3, ATTN_STEP_BLOCKS * SB_TILE, 0,
                       [_resident(sfx.shape), _resident(sb_masks.shape)], [sfx, sb_masks])
    out_b = _attn_call(
        _moba_kernel, "moba", qkv3, MOBA_STEP_BLOCKS * MOBA_BLOCK, 1,
        [pl.BlockSpec((None, MOBA_STEP_BLOCKS * MOBA_BLOCK, LANES), lambda b, p, i: (b, i, p)),
         pl.BlockSpec((None, nkb, LANES), lambda b, p, i: (b, 0, p)),
         _resident((seq, LANES))],
        [qf.reshape(b, seq, GROUP_WIDTH), km.reshape(b, nkb, GROUP_WIDTH), _block_onehot(seq)],
        bound_row=_score_bound_row(g_q_b[0], g_k_b[0]))
    x2 = _mix_ffn(x2, out_a.reshape(n, GROUP_WIDTH), out_b.reshape(n, GROUP_WIDTH), bf(w_out[0]),
                  row(norm_ffn2[0]), bf(ffn2_w_gate[0]), bf(ffn2_w_up[0]), bf(ffn2_w_down[0]))

    x2 = _ffn(x2, row(norm_ffn1[1]), bf(ffn1_w_gate[1]), bf(ffn1_w_up[1]), bf(ffn1_w_down[1]))
    w_cd = jnp.pad(bf(w_in_cd[0]), ((0, 0), (0, LANES - GROUP_HEADS)))
    b_f_l = jnp.pad(b_f[0].astype(F32), (0, LANES - GROUP_HEADS))[None, :]
    qkv, lf, dil = _proj_cd(
        x2, seq, row(norm_mix[1]), w_cd, _lane_gain(g_q_c[0]), _lane_gain(g_k_c[0]),
        _lane_gain(g_q_d[0]), _lane_gain(g_k_d[0]), b_f_l, cos, sin)
    qkv3 = qkv.reshape(b, seq, 3 * D_MIX)
    c = _cumsum(lf.reshape(b, seq, LANES))
    c_t = jnp.swapaxes(c[:, :, :GROUP_HEADS], 1, 2).reshape(b, N_PAIRS, HEADS_PER_BLOCK, seq)
    causal = jnp.asarray((_first_step_masks(ATTN_TILE, False) - 1.0) * MASK_BIG)
    out_c = _attn_call(
        _fox_kernel, "forgetting", qkv3, ATTN_STEP_BLOCKS * ATTN_TILE, 0,
        [pl.BlockSpec((None, ATTN_STEP_BLOCKS * ATTN_TILE, LANES), lambda b, p, i: (b, i, 0)),
         pl.BlockSpec((None, None, HEADS_PER_BLOCK, seq), lambda b, p, i: (b, p, 0, 0)),
         _resident(causal.shape)],
        [c, c_t, causal], bound_row=_score_bound_row(g_q_c[0], g_k_c[0]))
    table = jnp.asarray(_dilated_bias_table(ATTN_TILE, DILATED_CHUNK))
    dil_bound = _score_bound_row(g_q_d[0], g_k_d[0])

    def dilated_dense():
        return _attn_call(_dilated_kernel, "dilated", qkv3, ATTN_TILE, 1, [_resident(table.shape)],
                          [table], bound_row=dil_bound)

    if seq % DILATED_SLAB == 0 and seq >= 2 * DILATED_SLAB:
        out_d = lax.cond(dil_bound[0, 0] <= SHIFT_LIMIT,
                         lambda: _dilated_classes(dil.reshape(b, seq, 3 * GROUP_WIDTH), dil_bound),
                         dilated_dense)
    else:
        out_d = dilated_dense()
    x2 = _mix_ffn(x2, out_c.reshape(n, GROUP_WIDTH), out_d.reshape(n, GROUP_WIDTH), bf(w_out[1]),
                  row(norm_ffn2[1]), bf(ffn2_w_gate[1]), bf(ffn2_w_up[1]), bf(ffn2_w_down[1]))
    return x2.reshape(b, seq, d)
```

```python
import numpy as np
import jax
import jax.numpy as jnp
from jax import lax
from jax.experimental import pallas as pl
from jax.experimental.pallas import tpu as pltpu

F32 = jnp.float32
BF16 = jnp.bfloat16

D_MODEL = 1024
HEAD_DIM = 64
D_MIX = 1024
D_FF = 2816
GROUP_HEADS = 8
GROUP_WIDTH = GROUP_HEADS * HEAD_DIM
LANES = 128
HEADS_PER_BLOCK = LANES // HEAD_DIM
N_PAIRS = GROUP_WIDTH // LANES
RMS_EPS = 1e-6
ROPE_THETA = 10000.0
FFN_RES_WEIGHT = 0.5
ATTN_SCALE = HEAD_DIM ** -0.5
MOBA_BLOCK = 256
MOBA_TOPK = 3
DILATED_BRANCHES = ((128, 1), (512, 4), (2048, 16))
SKIP_LOG = 88.0
VMEM_LIMIT_BYTES = 56 * 1024 * 1024

TOKEN_TILE = 512
FF_CHUNK = 256
PROJ_CHUNK = 256
SB_TILE = 256
ATTN_TILE = 256
MOBA_CHUNK = 2 * MOBA_BLOCK
MOBA_STEP_BLOCKS = 4
ATTN_STEP_BLOCKS = 4
DILATED_CHUNK = 2 * ATTN_TILE
CLASS_TILE = 128
DILATED_SLAB = 16 * CLASS_TILE
MASK_BIG = 1e30
SHIFT_LIMIT = 40.0
CUMSUM_TILE = 256

_NT = (((1,), (1,)), ((), ()))


def _params(n_grid):
    return pltpu.CompilerParams(
        dimension_semantics=("arbitrary",) * n_grid,
        vmem_limit_bytes=VMEM_LIMIT_BYTES)


def _resident(shape):
    nd = len(shape)
    return pl.BlockSpec(shape, lambda *_: (0,) * nd, pipeline_mode=pl.Buffered(1))


def _dot(a, b):
    return jnp.dot(a, b, preferred_element_type=F32)


def _dot_nt(a, b):
    return lax.dot_general(a, b, _NT, preferred_element_type=F32)


def _split2(x):
    hi = x.astype(BF16)
    lo = (x - hi.astype(F32)).astype(BF16)
    return hi, lo


def _rms_rows(x, gain):
    return x * lax.rsqrt(jnp.mean(x * x, axis=-1, keepdims=True) + RMS_EPS) * gain


def _swiglu_residual(x, g_ref, wg_ref, wu_ref, wd_ref, h_ref):
    xn = _rms_rows(x, g_ref[...]).astype(BF16)
    for c in range(D_FF // FF_CHUNK):
        sl = slice(c * FF_CHUNK, (c + 1) * FF_CHUNK)
        gate = _dot(xn, wg_ref[:, sl])
        up = _dot(xn, wu_ref[:, sl])
        h_ref[:, sl] = (gate * jax.nn.sigmoid(gate) * up).astype(BF16)
    return x + FFN_RES_WEIGHT * _dot(h_ref[...], wd_ref[...])


def _ffn_kernel(x_ref, g_ref, wg_ref, wu_ref, wd_ref, o_ref, h_ref):
    o_ref[...] = _swiglu_residual(x_ref[...], g_ref, wg_ref, wu_ref, wd_ref, h_ref)


def _mix_ffn_kernel(x_ref, a_ref, b_ref, wo_ref, g_ref, wg_ref, wu_ref, wd_ref, o_ref, h_ref):
    x = (x_ref[...] + _dot(a_ref[...], wo_ref[0:GROUP_WIDTH, :])
         + _dot(b_ref[...], wo_ref[GROUP_WIDTH:D_MIX, :]))
    o_ref[...] = _swiglu_residual(x, g_ref, wg_ref, wu_ref, wd_ref, h_ref)


def _ffn_specs():
    return [pl.BlockSpec((1, D_MODEL), lambda i: (0, 0)),
            _resident((D_MODEL, D_FF)), _resident((D_MODEL, D_FF)), _resident((D_FF, D_MODEL))]


def _ffn(x2, gain, wg, wu, wd):
    n = x2.shape[0]
    row = pl.BlockSpec((TOKEN_TILE, D_MODEL), lambda i: (i, 0))
    return pl.pallas_call(
        _ffn_kernel,
        grid=(n // TOKEN_TILE,),
        in_specs=[row] + _ffn_specs(),
        out_specs=row,
        out_shape=jax.ShapeDtypeStruct((n, D_MODEL), F32),
        scratch_shapes=[pltpu.VMEM((TOKEN_TILE, D_FF), BF16)],
        compiler_params=_params(1),
        name="ffn",
    )(x2, gain, wg, wu, wd)


def _mix_ffn(x2, oa, ob, wo, gain, wg, wu, wd):
    n = x2.shape[0]
    row = pl.BlockSpec((TOKEN_TILE, D_MODEL), lambda i: (i, 0))
    half = pl.BlockSpec((TOKEN_TILE, GROUP_WIDTH), lambda i: (i, 0))
    return pl.pallas_call(
        _mix_ffn_kernel,
        grid=(n // TOKEN_TILE,),
        in_specs=[row, half, half, _resident((D_MIX, D_MODEL))] + _ffn_specs(),
        out_specs=row,
        out_shape=jax.ShapeDtypeStruct((n, D_MODEL), F32),
        scratch_shapes=[pltpu.VMEM((TOKEN_TILE, D_FF), BF16)],
        compiler_params=_params(1),
        name="mix_ffn",
    )(x2, oa, ob, wo, gain, wg, wu, wd)


def _head_sum_matrix():
    r = lax.broadcasted_iota(jnp.int32, (LANES, LANES), 0) // HEAD_DIM
    c = lax.broadcasted_iota(jnp.int32, (LANES, LANES), 1) // HEAD_DIM
    return jnp.where(r == c, 1.0, 0.0).astype(BF16)


def _head_rms(y, gain, hsum):
    ss = _dot((y * y).astype(BF16), hsum)
    return y * lax.rsqrt(ss * (1.0 / HEAD_DIM) + RMS_EPS) * gain


def _rope(y, cos, sin_signed, first_half):
    half = HEAD_DIM // 2
    partner = jnp.where(first_half, pltpu.roll(y, LANES - half, 1), pltpu.roll(y, half, 1))
    return y * cos + partner * sin_signed


def _proj_blocks(xn, w_ref, n_blocks):
    per = PROJ_CHUNK // LANES
    for c in range(0, n_blocks, per):
        width = min(per, n_blocks - c) * LANES
        y = _dot(xn, w_ref[:, c * LANES:c * LANES + width])
        for s in range(width // LANES):
            yield c + s, y[:, s * LANES:(s + 1) * LANES]


def _first_half_mask(rows):
    lane = lax.broadcasted_iota(jnp.int32, (rows, LANES), 1)
    return (lane % HEAD_DIM) < (HEAD_DIM // 2)


def _proj_ab_kernel(x_ref, g_ref, w_ref, gq_ref, gk_ref, cos_ref, sin_ref,
                    qkv_ref, qf_ref, km_ref):
    xn = _rms_rows(x_ref[...], g_ref[...]).astype(BF16)
    hsum = _head_sum_matrix()
    cos, sin = cos_ref[...], sin_ref[...]
    first_half = _first_half_mask(TOKEN_TILE)
    blocks_per_kind = D_MIX // LANES
    for cb, y in _proj_blocks(xn, w_ref, 3 * blocks_per_kind):
        kind, pp = divmod(cb, blocks_per_kind)
        if kind < 2 and pp >= N_PAIRS:
            gain = (gq_ref if kind == 0 else gk_ref)[...]
            y = _rope(_head_rms(y, gain, hsum), cos, sin, first_half)
            sl = slice((pp - N_PAIRS) * LANES, (pp - N_PAIRS + 1) * LANES)
            if kind == 0:
                qf_ref[:, sl] = y
            else:
                for jb in range(TOKEN_TILE // MOBA_BLOCK):
                    rows = y[jb * MOBA_BLOCK:(jb + 1) * MOBA_BLOCK, :]
                    km_ref[jb, :, sl] = jnp.mean(rows, axis=0, keepdims=True)
        if kind == 0:
            y = y * ATTN_SCALE
        qkv_ref[:, cb * LANES:(cb + 1) * LANES] = y.astype(BF16)


def _proj_cd_kernel(x_ref, g_ref, w_ref, gqc_ref, gkc_ref, gqd_ref, gkd_ref, bf_ref,
                    cos_ref, sin_ref, qkv_ref, lf_ref, dil_ref):
    xn = _rms_rows(x_ref[...], g_ref[...]).astype(BF16)
    hsum = _head_sum_matrix()
    cos, sin = cos_ref[...], sin_ref[...]
    first_half = _first_half_mask(TOKEN_TILE)
    blocks_per_kind = D_MIX // LANES
    for cb, y in _proj_blocks(xn, w_ref, 3 * blocks_per_kind + 1):
        kind, pp = divmod(cb, blocks_per_kind)
        if kind == 3:
            t = y + bf_ref[...]
            lf_ref[...] = jnp.minimum(t, 0.0) - jnp.log1p(jnp.exp(-jnp.abs(t)))
            continue
        if kind < 2:
            if pp < N_PAIRS:
                y = _head_rms(y, (gqc_ref if kind == 0 else gkc_ref)[...], hsum)
            else:
                y = _head_rms(y, (gqd_ref if kind == 0 else gkd_ref)[...], hsum)
                y = _rope(y, cos, sin, first_half)
        if kind == 0:
            y = y * ATTN_SCALE
        qkv_ref[:, cb * LANES:(cb + 1) * LANES] = y.astype(BF16)
        if pp >= N_PAIRS:
            col = (kind * N_PAIRS + pp - N_PAIRS) * LANES
            dil_ref[:, col:col + LANES] = y


def _proj_common_specs(seq, w_cols):
    row = pl.BlockSpec((TOKEN_TILE, D_MODEL), lambda i: (i, 0))
    tiles_per_seq = seq // TOKEN_TILE
    table = pl.BlockSpec((TOKEN_TILE, LANES), lambda i: (i % tiles_per_seq, 0))
    lane_vec = pl.BlockSpec((1, LANES), lambda i: (0, 0))
    gain = pl.BlockSpec((1, D_MODEL), lambda i: (0, 0))
    return row, gain, _resident((D_MODEL, w_cols)), lane_vec, table


def _proj_ab(x2, seq, gain, w, gq, gk, cos, sin):
    n = x2.shape[0]
    row, gspec, wspec, lane_vec, table = _proj_common_specs(seq, 3 * D_MIX)
    blocks = TOKEN_TILE // MOBA_BLOCK
    return pl.pallas_call(
        _proj_ab_kernel,
        grid=(n // TOKEN_TILE,),
        in_specs=[row, gspec, wspec, lane_vec, lane_vec, table, table],
        out_specs=[pl.BlockSpec((TOKEN_TILE, 3 * D_MIX), lambda i: (i, 0)),
                   pl.BlockSpec((TOKEN_TILE, GROUP_WIDTH), lambda i: (i, 0)),
                   pl.BlockSpec((blocks, 1, GROUP_WIDTH), lambda i: (i, 0, 0))],
        out_shape=[jax.ShapeDtypeStruct((n, 3 * D_MIX), BF16),
                   jax.ShapeDtypeStruct((n, GROUP_WIDTH), F32),
                   jax.ShapeDtypeStruct((n // MOBA_BLOCK, 1, GROUP_WIDTH), F32)],
        compiler_params=_params(1),
        name="proj_ab",
    )(x2, gain, w, gq, gk, cos, sin)


def _proj_cd(x2, seq, gain, w, gqc, gkc, gqd, gkd, bf, cos, sin):
    n = x2.shape[0]
    row, gspec, wspec, lane_vec, table = _proj_common_specs(seq, 3 * D_MIX + LANES)
    return pl.pallas_call(
        _proj_cd_kernel,
        grid=(n // TOKEN_TILE,),
        in_specs=[row, gspec, wspec, lane_vec, lane_vec, lane_vec, lane_vec, lane_vec, table, table],
        out_specs=[pl.BlockSpec((TOKEN_TILE, 3 * D_MIX), lambda i: (i, 0)),
                   pl.BlockSpec((TOKEN_TILE, LANES), lambda i: (i, 0)),
                   pl.BlockSpec((TOKEN_TILE, 3 * GROUP_WIDTH), lambda i: (i, 0))],
        out_shape=[jax.ShapeDtypeStruct((n, 3 * D_MIX), BF16),
                   jax.ShapeDtypeStruct((n, LANES), F32),
                   jax.ShapeDtypeStruct((n, 3 * GROUP_WIDTH), F32)],
        compiler_params=_params(1),
        name="proj_cd",
    )(x2, gain, w, gqc, gkc, gqd, gkd, bf, cos, sin)


def _cumsum_kernel(lf_ref, c_ref, carry_ref):
    @pl.when(pl.program_id(1) == 0)
    def _():
        carry_ref[...] = jnp.zeros_like(carry_ref)

    t = CUMSUM_TILE
    r = lax.broadcasted_iota(jnp.int32, (t, t), 0)
    c = lax.broadcasted_iota(jnp.int32, (t, t), 1)
    lower = jnp.where(c <= r, 1.0, 0.0).astype(BF16)
    lf = lf_ref[...]
    hi = lf.astype(BF16)
    mid, lo = _split2(lf - hi.astype(F32))
    out = _dot(lower, hi) + _dot(lower, mid) + _dot(lower, lo) + carry_ref[0:1, :]
    c_ref[...] = out
    carry_ref[0:1, :] = out[t - 1:t, :]


def _cumsum(lf3):
    b, s, _ = lf3.shape
    blk = pl.BlockSpec((None, CUMSUM_TILE, LANES), lambda bi, i: (bi, i, 0))
    return pl.pallas_call(
        _cumsum_kernel,
        grid=(b, s // CUMSUM_TILE),
        in_specs=[blk],
        out_specs=blk,
        out_shape=jax.ShapeDtypeStruct(lf3.shape, F32),
        scratch_shapes=[pltpu.VMEM((8, LANES), F32)],
        compiler_params=_params(2),
        name="forget_cumsum",
    )(lf3)


def _pair_queries(q, rows):
    head = lax.broadcasted_iota(jnp.int32, (rows, LANES), 1) // HEAD_DIM
    return head, [jnp.where(head == h, q, jnp.zeros_like(q)) for h in range(HEADS_PER_BLOCK)]


def _softmax_step(s, m, l, acc, vb):
    m_new = jnp.maximum(m, jnp.max(s, axis=1, keepdims=True))
    alpha = jnp.exp(m - m_new)
    p = jnp.exp(s - m_new)
    l = alpha * l + jnp.sum(p, axis=1, keepdims=True)
    acc = alpha * acc + _dot(p.astype(BF16), vb)
    return m_new, l, acc


def _softmax_init(rows):
    return (jnp.full((rows, 1), -jnp.inf, F32), jnp.zeros((rows, 1), F32),
            jnp.zeros((rows, LANES), F32))


def _loop_grouped(n, body, init, groups=(2, 1)):
    done, state = 0, init
    for g in groups:
        def grouped(c, st, g=g, done=done):
            for u in range(g):
                st = body(done + g * c + u, st)
            return st
        count = (n - done) // g
        state = lax.fori_loop(0, count, grouped, state)
        done = done + count * g
    return state


def _causal_bias(t):
    row = lax.broadcasted_iota(jnp.int32, (t, t), 0)
    col = lax.broadcasted_iota(jnp.int32, (t, t), 1)
    return jnp.where(col <= row, 0.0, -MASK_BIG)


def _values_with_ones(vb):
    head = lax.broadcasted_iota(jnp.int32, vb.shape, 1) // HEAD_DIM
    return [jnp.where(head == h, vb, jnp.ones_like(vb)) for h in range(HEADS_PER_BLOCK)]


def _normalize_heads(head, accs):
    outs = [acc / pltpu.roll(acc, HEAD_DIM, 1) for acc in accs]
    return jnp.where(head == 0, outs[0], outs[1])


def _sb_kernel(q_ref, sfx_ref, mask_ref, k_ref, v_ref, o_ref):
    t = SB_TILE
    outs = []
    for s in range(ATTN_STEP_BLOCKS):
        i = pl.program_id(2) * ATTN_STEP_BLOCKS + s
        outs.append(_sb_block(i, q_ref[s * t:(s + 1) * t, :], sfx_ref, mask_ref, k_ref, v_ref))
    outs = [finish() for finish in outs]
    o_ref[...] = jnp.concatenate(outs, axis=0).astype(o_ref.dtype)


def _sb_block(i, q, sfx_ref, mask_ref, k_ref, v_ref):
    t = SB_TILE
    head, qhs = _pair_queries(q, t)
    sfx = sfx_ref[...]
    lanes_of = lambda x: jnp.concatenate([x] * (t // LANES), axis=1)

    def block_sums(log_keep):
        hi, lo = _split2(log_keep)
        return _dot(jnp.concatenate([hi, lo], axis=1), sfx)

    def log_gates(qh, kb):
        z = _dot_nt(qh, kb)
        soft = jnp.log(1.0 + jnp.exp(-jnp.abs(z)))
        log_beta = jnp.minimum(z, 0.0) - soft
        return log_beta, log_beta - z

    which = jnp.minimum(i, 1)
    start = pl.multiple_of(jnp.maximum(i - 1, 0) * t, t)
    strict = mask_ref[which, 0]
    strict_neg = mask_ref[which, 1]
    kb = k_ref[pl.ds(start, 2 * t), :]
    vb = v_ref[pl.ds(start, 2 * t), :]
    carry = (i - 2,)
    for qh in qhs:
        log_beta, log_keep = log_gates(qh, kb)
        log_keep = log_keep * strict
        late = block_sums(log_keep[:, t:])
        early = block_sums(log_keep[:, :t])
        between = jnp.concatenate([early[:, :t] + lanes_of(late[:, t:]), late[:, :t]], axis=1)
        w = jnp.exp(log_beta + between + strict_neg)
        carry += (late[:, t:] + early[:, t:], _dot(w.astype(BF16), vb))

    def step(j, run, acc, qh):
        start = pl.multiple_of(j * t, t)
        log_beta, log_keep = log_gates(qh, k_ref[pl.ds(start, t), :])
        sums = block_sums(log_keep)
        w = jnp.exp(log_beta + lanes_of(run) + sums[:, :t])
        return run + sums[:, t:], acc + _dot(w.astype(BF16), v_ref[pl.ds(start, t), :])

    def cond(carry):
        return (carry[0] >= 0) & (jnp.maximum(jnp.max(carry[1]), jnp.max(carry[3])) > -SKIP_LOG)

    def body(carry):
        out = (carry[0] - 1,)
        for h, qh in enumerate(qhs):
            out += step(carry[0], carry[1 + 2 * h], carry[2 + 2 * h], qh)
        return out

    def finish():
        done = lax.while_loop(cond, body, carry)
        return jnp.where(head == 0, done[2], done[4])

    return finish


def _score_bound_row(g_q, g_k):
    bound = 1.05 * HEAD_DIM * ATTN_SCALE * jnp.max(jnp.abs(g_q)) * jnp.max(jnp.abs(g_k))
    return jnp.full((1, LANES), bound, F32)


def _moba_kernel(bound_ref, q_ref, qf_ref, km_ref, onehot_ref, k_ref, v_ref, o_ref):
    t, ch, nq = MOBA_BLOCK, MOBA_CHUNK, MOBA_STEP_BLOCKS
    rows = nq * t
    per = ch // t
    first_own = pl.program_id(2) * nq
    nkb = km_ref.shape[0]
    q, qf = q_ref[...], qf_ref[...]
    head, qhs = _pair_queries(q, rows)
    bound = bound_ref[...][:, 0:1]
    lane = lax.broadcasted_iota(jnp.int32, (rows, LANES), 1)
    km_hi, km_lo = _split2(km_ref[...])
    blk_t = lax.broadcasted_iota(jnp.int32, (nkb, rows), 0)
    blk_tf = blk_t.astype(F32)
    own_t = first_own + lax.broadcasted_iota(jnp.int32, (nkb, rows), 1) // t
    own_row = own_t[0:1, :]
    causal = _causal_bias(t)
    n_chunks = (first_own + nq - 1 + per - 1) // per

    blocked = []
    for h in range(HEADS_PER_BLOCK):
        q_hi, q_lo = _split2(jnp.where(head == h, qf, 0.0))
        gate = _dot_nt(km_hi, q_hi) + _dot_nt(km_lo, q_hi) + _dot_nt(km_hi, q_lo)
        gate = jnp.where(blk_t < own_t, gate, -jnp.inf)
        sel = jnp.zeros((nkb, rows), F32)
        for r in range(MOBA_TOPK):
            best = jnp.max(gate, axis=0, keepdims=True)
            first = jnp.min(jnp.where(gate == best, blk_tf, float(nkb)), axis=0, keepdims=True)
            hit = blk_tf == first
            sel = sel + jnp.where(hit, jnp.where(own_row > r, 1.0, 0.0), 0.0)
            gate = jnp.where(hit, -jnp.inf, gate)
        neg = jnp.concatenate([(sel - 1.0) * MASK_BIG, jnp.zeros((LANES - nkb, rows), F32)], axis=0)
        blocked.append(neg.T)

    def rhs_of(start, count):
        return jnp.concatenate([k_ref[pl.ds(start, count), :], onehot_ref[pl.ds(start, count), :]], axis=1)

    def own_blocks(fn):
        parts = [fn(slice(s * t, (s + 1) * t), pl.multiple_of((first_own + s) * t, t)) for s in range(nq)]
        return tuple(jnp.concatenate(p, axis=0) for p in zip(*parts))

    def shifted():
        lhs, lhs_own = [], []
        for h, qh in enumerate(qhs):
            extra = jnp.where(lane == LANES - 1, -bound, blocked[h])
            lhs.append(jnp.concatenate([qh, extra.astype(BF16)], axis=1))
            extra_own = jnp.where(lane == LANES - 1, -bound, 0.0)
            lhs_own.append(jnp.concatenate([qh, extra_own.astype(BF16)], axis=1))

        def own_pass(rs, start):
            rhs_own = rhs_of(start, t)
            v_owns = _values_with_ones(v_ref[pl.ds(start, t), :])
            return tuple(_dot(jnp.exp(_dot_nt(a[rs], rhs_own) + causal).astype(BF16), vh)
                         for a, vh in zip(lhs_own, v_owns))

        accs = own_blocks(own_pass)

        def sweep(c, accs, row0):
            start = pl.multiple_of(c * ch, ch)
            rhs = rhs_of(start, ch)
            vhs = _values_with_ones(v_ref[pl.ds(start, ch), :])
            new = tuple(acc[row0:] + _dot(jnp.exp(_dot_nt(a[row0:], rhs)).astype(BF16), vh)
                        for a, vh, acc in zip(lhs, vhs, accs))
            if row0 == 0:
                return new
            return tuple(jnp.concatenate([acc[:row0], part], axis=0) for acc, part in zip(accs, new))

        shared = first_own // per
        key_head = lax.broadcasted_iota(jnp.int32, (LANES, ch), 0) // HEAD_DIM

        def sweep_all(c, accs_t):
            start = pl.multiple_of(c * ch, ch)
            rhs = rhs_of(start, ch)
            v_t = v_ref[pl.ds(start, ch), :].astype(F32).T.astype(BF16)
            return tuple(
                acc_t + _dot(jnp.where(key_head == h, v_t, jnp.ones_like(v_t)),
                             jnp.exp(_dot_nt(rhs, a)).astype(BF16))
                for h, (a, acc_t) in enumerate(zip(lhs, accs_t)))

        accs_t = _loop_grouped(shared, sweep_all, (jnp.zeros((LANES, rows), F32),) * HEADS_PER_BLOCK,
                               groups=(2, 1))
        accs = tuple(acc + acc_t.T for acc, acc_t in zip(accs, accs_t))
        for extra_chunk in range((nq - 1 + per - 1) // per):
            accs = sweep(shared + extra_chunk, accs, (extra_chunk * per + 1) * t)
        return _normalize_heads(head, accs)

    def running_max():
        lhs = [jnp.concatenate([qh, neg.astype(BF16)], axis=1) for qh, neg in zip(qhs, blocked)]

        def own_pass(rs, start):
            k_own, v_own = k_ref[pl.ds(start, t), :], v_ref[pl.ds(start, t), :]
            out = ()
            for qh in qhs:
                out += _softmax_step(_dot_nt(qh[rs], k_own) + causal, *_softmax_init(t), v_own)
            return out

        flat = own_blocks(own_pass)
        states = tuple(flat[3 * h:3 * h + 3] for h in range(HEADS_PER_BLOCK))

        def body(c, states):
            start = pl.multiple_of(c * ch, ch)
            rhs = rhs_of(start, ch)
            vb = v_ref[pl.ds(start, ch), :]
            return tuple(_softmax_step(_dot_nt(a, rhs), *st, vb) for a, st in zip(lhs, states))

        states = lax.fori_loop(0, n_chunks, body, states)
        outs = [acc / l for _, l, acc in states]
        return jnp.where(head == 0, outs[0], outs[1])

    out = lax.cond(jnp.max(bound) <= SHIFT_LIMIT, shifted, running_max)
    o_ref[...] = out.astype(o_ref.dtype)


def _fox_kernel(bound_ref, q_ref, c_ref, ct_ref, causal_ref, k_ref, v_ref, o_ref):
    t = ATTN_TILE
    qk_bound = bound_ref[...][:, 0:1]
    blocks = []
    for s in range(ATTN_STEP_BLOCKS):
        rs = slice(s * t, (s + 1) * t)
        blocks.append(_fox_block(pl.program_id(2) * ATTN_STEP_BLOCKS + s, q_ref[rs, :], c_ref[rs, :],
                                 qk_bound, ct_ref, causal_ref, k_ref, v_ref))

    def shifted():
        finishers = [start() for start, _ in blocks]
        return jnp.concatenate([finish() for finish in finishers], axis=0)

    def running_max():
        return jnp.concatenate([run() for _, run in blocks], axis=0)

    out = lax.cond(jnp.max(qk_bound) <= SHIFT_LIMIT, shifted, running_max)
    o_ref[...] = out.astype(o_ref.dtype)


def _fox_block(i, q, c_blk, qk_bound, ct_ref, causal_ref, k_ref, v_ref):
    t = ATTN_TILE
    pair = pl.program_id(1)
    head, qhs = _pair_queries(q, t)
    lane = lax.broadcasted_iota(jnp.int32, (t, LANES), 1)
    c_qs = [jnp.sum(jnp.where(lane == HEADS_PER_BLOCK * pair + h, c_blk, 0.0), axis=1, keepdims=True)
            for h in range(HEADS_PER_BLOCK)]
    bounds = [qk_bound] * HEADS_PER_BLOCK

    def block_min_c(h, j):
        start = pl.multiple_of(jnp.maximum(j, 0) * t, t)
        return jnp.min(ct_ref[h:h + 1, pl.ds(start, t)])

    def shifted():
        lhs = [jnp.concatenate([qh, jnp.where(lane == LANES - 1, -bounds[h], 0.0).astype(BF16)], axis=1)
               for h, qh in enumerate(qhs)]
        ones_col = {rows: jnp.where(lax.broadcasted_iota(jnp.int32, (rows, LANES), 1) == LANES - 1,
                                    1.0, 0.0).astype(BF16) for rows in (t, 2 * t)}
        level = [jnp.max(c_qs[h] + 2.0 * bounds[h]) for h in range(HEADS_PER_BLOCK)]

        def attend(start, rows, bias, accs):
            rhs = jnp.concatenate([k_ref[pl.ds(start, rows), :], ones_col[rows]], axis=1)
            vhs = _values_with_ones(v_ref[pl.ds(start, rows), :])
            out = ()
            for h, a in enumerate(lhs):
                s = _dot_nt(a, rhs) + (c_qs[h] - ct_ref[h:h + 1, pl.ds(start, rows)])
                p = jnp.exp(s if bias is None else s + bias)
                out += (accs[h] + _dot(p.astype(BF16), vhs[h]),)
            return out

        zeros = jnp.zeros((t, LANES), F32)
        start = pl.multiple_of(jnp.maximum(i - 1, 0) * t, t)
        carry = (i - 2,) + attend(start, 2 * t, causal_ref[jnp.minimum(i, 1)], (zeros,) * 2)

        def cond(carry):
            j = carry[0]
            margin = [level[h] - block_min_c(h, j) for h in range(HEADS_PER_BLOCK)]
            return (j >= 0) & (jnp.maximum(margin[0], margin[1]) > -SKIP_LOG)

        def body(carry):
            start = pl.multiple_of(carry[0] * t, t)
            return (carry[0] - 1,) + attend(start, t, None, carry[1:])

        return lambda: _normalize_heads(head, lax.while_loop(cond, body, carry)[1:])

    def running_max():
        reach =[bounds[h] + c_qs[h] for h in range(HEADS_PER_BLOCK)]

        def step(j, states, extra):
            start = pl.multiple_of(j * t, t)
            kb = k_ref[pl.ds(start, t), :]
            vb = v_ref[pl.ds(start, t), :]
            out = ()
            for h, qh in enumerate(qhs):
                s = _dot_nt(qh, kb) + (c_qs[h] - ct_ref[h:h + 1, pl.ds(start, t)])
                if extra is not None:
                    s = s + extra
                out += _softmax_step(s, *states[3 * h:3 * h + 3], vb)
            return out

        carry = (i - 1,) + step(i, _softmax_init(t) * HEADS_PER_BLOCK, _causal_bias(t))

        def cond(carry):
            j = carry[0]
            margin = [jnp.max(reach[h] - carry[1 + 3 * h]) - block_min_c(h, j)
                      for h in range(HEADS_PER_BLOCK)]
            return (j >= 0) & (jnp.maximum(margin[0], margin[1]) > -SKIP_LOG)

        def body(carry):
            return (carry[0] - 1,) + step(carry[0], carry[1:], None)

        carry = lax.while_loop(cond, body, carry)
        outs = [carry[3 + 3 * h] / carry[2 + 3 * h] for h in range(HEADS_PER_BLOCK)]
        return jnp.where(head == 0, outs[0], outs[1])

    return shifted, running_max


def _dilated_kernel(bound_ref, q_ref, tbl_ref, k_ref, v_ref, o_ref):
    t, ch = ATTN_TILE, DILATED_CHUNK
    i = pl.program_id(2)
    q = q_ref[...]
    head, qhs = _pair_queries(q, t)
    bound = bound_ref[...][:, 0:1]
    half = i // 2
    parity = i - 2 * half
    n_chunks = jnp.minimum(half + 1, tbl_ref.shape[0] // 2)

    def chunk(c):
        start = pl.multiple_of((half - c) * ch, ch)
        return k_ref[pl.ds(start, ch), :], v_ref[pl.ds(start, ch), :], tbl_ref[parity + 2 * c]

    def shifted():
        lane = lax.broadcasted_iota(jnp.int32, (t, LANES), 1)
        lhs = [jnp.concatenate(
            [qh, jnp.where(lane == LANES - 1, -bound, 0.0).astype(BF16)],
            axis=1) for h, qh in enumerate(qhs)]
        ones_col = jnp.where(lax.broadcasted_iota(jnp.int32, (ch, LANES), 1) == LANES - 1, 1.0, 0.0).astype(BF16)

        def body(c, states):
            kb, vb, bias = chunk(c)
            rhs = jnp.concatenate([kb, ones_col], axis=1)
            out = ()
            for a, (l, acc) in zip(lhs, states):
                p = jnp.exp(_dot_nt(a, rhs) + bias)
                for c0 in range(0, ch, LANES):
                    l = l + p[:, c0:c0 + LANES]
                out += ((l, acc + _dot(p.astype(BF16), vb)),)
            return out

        zeros = jnp.zeros((t, LANES), F32)
        states = _loop_grouped(n_chunks, body, ((zeros, zeros),) * HEADS_PER_BLOCK)
        outs = [acc / jnp.sum(l, axis=1, keepdims=True) for l, acc in states]
        return jnp.where(head == 0, outs[0], outs[1])

    def running_max():
        def body(c, states):
            kb, vb, bias = chunk(c)
            return tuple(_softmax_step(_dot_nt(qh, kb) + bias, *st, vb) for qh, st in zip(qhs, states))

        states = lax.fori_loop(0, n_chunks, body, (_softmax_init(t),) * HEADS_PER_BLOCK)
        outs = [acc / l for _, l, acc in states]
        return jnp.where(head == 0, outs[0], outs[1])

    out = lax.cond(jnp.max(bound) <= SHIFT_LIMIT, shifted, running_max)
    o_ref[...] = out.astype(o_ref.dtype)


def _dilated_class_kernel(shift_ref, band_ref, q_ref, k_ref, v_ref, o_ref, tot_ref):
    ct = CLASS_TILE
    slab = pl.program_id(2)
    head = lax.broadcasted_iota(jnp.int32, (ct, LANES), 1) // HEAD_DIM
    shift = shift_ref[...][:, 0:1]
    band_first = band_ref[0] - shift
    band = band_ref[1] - shift
    band_edge = jnp.where(slab == 0, band_first, band)

    def rows(ref, start, count, d):
        return ref[pl.ds(start, count), :] if d == 1 else ref[pl.ds(start, count, stride=d), :]

    for branch, (window, d) in enumerate(DILATED_BRANCHES):
        assert window == ct * d and DILATED_SLAB % (ct * d) == 0
        for r in range(d):
            for u in range(DILATED_SLAB // (ct * d)):
                row0 = r + d * ct * u
                qb = rows(q_ref, row0, ct, d).astype(BF16)
                key0 = slab * DILATED_SLAB + (row0 - d * ct)
                start = jnp.maximum(key0, r) if u == 0 else key0
                kb = rows(k_ref, start, 2 * ct, d).astype(BF16)
                vhs = _values_with_ones(rows(v_ref, start, 2 * ct, d).astype(BF16))
                for h in range(HEADS_PER_BLOCK):
                    qh = jnp.where(head == h, qb, jnp.zeros_like(qb))
                    p = jnp.exp(_dot_nt(qh, kb) + (band_edge if u == 0 else band))
                    acc = _dot(p.astype(BF16), vhs[h])
                    if branch == 0:
                        tot_ref[h, pl.ds(row0, ct), :] = acc
                    elif d == 1:
                        tot_ref[h, pl.ds(row0, ct), :] += acc
                    else:
                        tot_ref[h, pl.ds(row0, ct, stride=d), :] += acc

    slab_head = lax.broadcasted_iota(jnp.int32, (DILATED_SLAB, LANES), 1) // HEAD_DIM
    out = _normalize_heads(slab_head, [tot_ref[h] for h in range(HEADS_PER_BLOCK)])
    o_ref[...] = out.astype(o_ref.dtype)


def _dilated_classes(dil3, shift_row):
    b, seq, _ = dil3.shape
    ct = CLASS_TILE
    r = np.arange(ct)[:, None]
    c = np.arange(2 * ct)[None, :]
    band = np.stack([c <= r, (c >= r) & (c <= r + ct)]).astype(np.float32)
    band = jnp.asarray((band - 1.0) * MASK_BIG)
    return pl.pallas_call(
        _dilated_class_kernel,
        grid=(b, N_PAIRS, seq // DILATED_SLAB),
        in_specs=[pl.BlockSpec((1, LANES), lambda b, p, i: (0, 0)),
                  _resident(band.shape),
                  pl.BlockSpec((None, DILATED_SLAB, LANES), lambda b, p, i: (b, i, p)),
                  pl.BlockSpec((None, seq, LANES), lambda b, p, i: (b, 0, N_PAIRS + p)),
                  pl.BlockSpec((None, seq, LANES), lambda b, p, i: (b, 0, 2 * N_PAIRS + p))],
        out_specs=pl.BlockSpec((None, DILATED_SLAB, LANES), lambda b, p, i: (b, i, p)),
        out_shape=jax.ShapeDtypeStruct((b, seq, GROUP_WIDTH), BF16),
        scratch_shapes=[pltpu.VMEM((HEADS_PER_BLOCK, DILATED_SLAB, LANES), F32)],
        compiler_params=_params(3),
        name="dilated_classes",
    )(shift_row, band, dil3, dil3, dil3)


def _qkv_specs(seq, tq, group):
    blocks_per_kind = D_MIX // LANES
    off = group * N_PAIRS
    q = pl.BlockSpec((None, tq, LANES), lambda b, p, i: (b, i, off + p))
    k = pl.BlockSpec((None, seq, LANES), lambda b, p, i: (b, 0, blocks_per_kind + off + p))
    v = pl.BlockSpec((None, seq, LANES), lambda b, p, i: (b, 0, 2 * blocks_per_kind + off + p))
    return q, k, v


def _attn_call(kernel_fn, name, qkv3, tq, group, extra_specs, extra_args, bound_row=None):
    b, seq, _ = qkv3.shape
    q, k, v = _qkv_specs(seq, tq, group)
    lead_specs = [] if bound_row is None else [pl.BlockSpec((1, LANES), lambda b, p, i: (0, 0))]
    lead_args = [] if bound_row is None else [bound_row]
    return pl.pallas_call(
        kernel_fn,
        grid=(b, N_PAIRS, seq // tq),
        in_specs=lead_specs + [q] + list(extra_specs) + [k, v],
        out_specs=pl.BlockSpec((None, tq, LANES), lambda b, p, i: (b, i, p)),
        out_shape=jax.ShapeDtypeStruct((b, seq, GROUP_WIDTH), BF16),
        compiler_params=_params(3),
        name=name,
    )(*lead_args, qkv3, *extra_args, qkv3, qkv3)


def _dilated_bias_table(t, ch):
    reach = max(w for w, _ in DILATED_BRANCHES)
    n = 2 * ((reach // t + 1 + ch // t) // 2)
    r = np.arange(t)[:, None]
    c = np.arange(ch)[None, :]
    out = np.zeros((n, t, ch), np.float32)
    for e in range(n):
        dist = e * t + r - c
        count = np.zeros((t, ch), np.float32)
        for window, dil in DILATED_BRANCHES:
            count += (dist >= 0) & (dist <= window) & (dist % dil == 0)
        out[e] = np.where(count > 0, np.log(np.maximum(count, 1.0)), -MASK_BIG)
    return out


def _suffix_matrix(t):
    j = np.arange(2 * t)[:, None] % t
    s = np.arange(t + LANES)[None, :]
    return jnp.asarray(((s >= t) | (j > s)).astype(np.float32), dtype=BF16)


def _first_step_masks(t, strict):
    r = np.arange(t)[:, None]
    c = np.arange(2 * t)[None, :]
    return np.stack([(c < r + d) if strict else (c <= r + d) for d in (0, t)]).astype(np.float32)


def _block_onehot(seq):
    assert seq // MOBA_BLOCK < LANES
    out = np.zeros((seq, LANES), np.float32)
    out[np.arange(seq), np.arange(seq) // MOBA_BLOCK] = 1.0
    out[:, LANES - 1] = 1.0
    return jnp.asarray(out, dtype=BF16)


def _rope_tables(seq):
    inv = 1.0 / (ROPE_THETA ** (jnp.arange(0, HEAD_DIM, 2, dtype=F32) / HEAD_DIM))
    ang = jnp.arange(seq, dtype=F32)[:, None] * inv[None, :]
    cos, sin = jnp.cos(ang), jnp.sin(ang)
    reps = LANES // HEAD_DIM
    cos_l = jnp.tile(jnp.concatenate([cos, cos], axis=1), (1, reps))
    sin_l = jnp.tile(jnp.concatenate([-sin, sin], axis=1), (1, reps))
    return cos_l, sin_l


def _lane_gain(g):
    return jnp.tile(g.astype(F32), LANES // HEAD_DIM)[None, :]


def kernel(x, norm_ffn1, ffn1_w_gate, ffn1_w_up, ffn1_w_down, norm_mix, w_in_ab, g_q_b, g_k_b,
           w_in_cd, b_f, g_q_c, g_k_c, g_q_d, g_k_d, w_out, norm_ffn2, ffn2_w_gate, ffn2_w_up,
           ffn2_w_down):
    b, seq, d = x.shape
    n = b * seq
    assert d == D_MODEL and n % TOKEN_TILE == 0 and seq % TOKEN_TILE == 0
    assert seq % MOBA_CHUNK == 0 and seq % SB_TILE == 0 and seq % DILATED_CHUNK == 0
    cos, sin = _rope_tables(seq)
    bf = lambda w: w.astype(BF16)
    row = lambda g: g.astype(F32)[None, :]
    x2 = x.reshape(n, d)

    x2 = _ffn(x2, row(norm_ffn1[0]), bf(ffn1_w_gate[0]), bf(ffn1_w_up[0]), bf(ffn1_w_down[0]))
    qkv, qf, km = _proj_ab(x2, seq, row(norm_mix[0]), bf(w_in_ab[0]),
                           _lane_gain(g_q_b[0]), _lane_gain(g_k_b[0]), cos, sin)
    qkv3 = qkv.reshape(b, seq, 3 * D_MIX)
    nkb = seq // MOBA_BLOCK
    sfx =_suffix_matrix(SB_TILE)
    strict = _first_step_masks(SB_TILE, True)
    sb_masks = jnp.asarray(np.stack([strict, (strict - 1.0) * MASK_BIG], axis=1))
    out_a = _attn_call(_sb_kernel, "stick_breaking", qkv3, ATTN_STEP_BLOCKS * SB_TILE, 0,
                       [_resident(sfx.shape), _resident(sb_masks.shape)], [sfx, sb_masks])
    out_b = _attn_call(
        _moba_kernel, "moba", qkv3, MOBA_STEP_BLOCKS * MOBA_BLOCK, 1,
        [pl.BlockSpec((None, MOBA_STEP_BLOCKS * MOBA_BLOCK, LANES), lambda b, p, i: (b, i, p)),
         pl.BlockSpec((None, nkb, LANES), lambda b, p, i: (b, 0, p)),
         _resident((seq, LANES))],
        [qf.reshape(b, seq, GROUP_WIDTH), km.reshape(b, nkb, GROUP_WIDTH), _block_onehot(seq)],
        bound_row=_score_bound_row(g_q_b[0], g_k_b[0]))
    x2 = _mix_ffn(x2, out_a.reshape(n, GROUP_WIDTH), out_b.reshape(n, GROUP_WIDTH), bf(w_out[0]),
                  row(norm_ffn2[0]), bf(ffn2_w_gate[0]), bf(ffn2_w_up[0]), bf(ffn2_w_down[0]))

    x2 = _ffn(x2, row(norm_ffn1[1]), bf(ffn1_w_gate[1]), bf(ffn1_w_up[1]), bf(ffn1_w_down[1]))
    w_cd = jnp.pad(bf(w_in_cd[0]), ((0, 0), (0, LANES - GROUP_HEADS)))
    b_f_l = jnp.pad(b_f[0].astype(F32), (0, LANES - GROUP_HEADS))[None, :]
    qkv, lf, dil = _proj_cd(
        x2, seq, row(norm_mix[1]), w_cd, _lane_gain(g_q_c[0]), _lane_gain(g_k_c[0]),
        _lane_gain(g_q_d[0]), _lane_gain(g_k_d[0]), b_f_l, cos, sin)
    qkv3 = qkv.reshape(b, seq, 3 * D_MIX)
    c = _cumsum(lf.reshape(b, seq, LANES))
    c_t = jnp.swapaxes(c[:, :, :GROUP_HEADS], 1, 2).reshape(b, N_PAIRS, HEADS_PER_BLOCK, seq)
    causal = jnp.asarray((_first_step_masks(ATTN_TILE, False) - 1.0) * MASK_BIG)
    out_c = _attn_call(
        _fox_kernel, "forgetting", qkv3, ATTN_STEP_BLOCKS * ATTN_TILE, 0,
        [pl.BlockSpec((None, ATTN_STEP_BLOCKS * ATTN_TILE, LANES), lambda b, p, i: (b, i, 0)),
         pl.BlockSpec((None, None, HEADS_PER_BLOCK, seq), lambda b, p, i: (b, p, 0, 0)),
         _resident(causal.shape)],
        [c, c_t, causal], bound_row=_score_bound_row(g_q_c[0], g_k_c[0]))
    table = jnp.asarray(_dilated_bias_table(ATTN_TILE, DILATED_CHUNK))
    dil_bound = _score_bound_row(g_q_d[0], g_k_d[0])

    def dilated_dense():
        return _attn_call(_dilated_kernel, "dilated", qkv3, ATTN_TILE, 1, [_resident(table.shape)],
                          [table], bound_row=dil_bound)

    if seq % DILATED_SLAB == 0 and seq >= 2 * DILATED_SLAB:
        out_d = lax.cond(dil_bound[0, 0] <= SHIFT_LIMIT,
                         lambda: _dilated_classes(dil.reshape(b, seq, 3 * GROUP_WIDTH), dil_bound),
                         dilated_dense)
    else:
        out_d = dilated_dense()
    x2 = _mix_ffn(x2, out_c.reshape(n, GROUP_WIDTH), out_d.reshape(n, GROUP_WIDTH), bf(w_out[1]),
                  row(norm_ffn2[1]), bf(ffn2_w_gate[1]), bf(ffn2_w_up[1]), bf(ffn2_w_down[1]))
    return x2.reshape(b, seq, d)
```
